```python
import jax
import jax.numpy as jnp
from jax import lax
import numpy as np

D_MODEL = 2048
BATCH = 2
SEQ = 4096
DEPTH = 4

GRID_W = 64
CTX_LEN = 256
EPS = 1e-6
CONV_DIM = D_MODEL // 2
CONV_WIDTH = 31
SGU_DIM = D_MODEL // 2
SGU_CHUNK = 128
SGU_GROUPS = SGU_DIM // 128
NA_HEAD_DIM = 64
NA_HEADS = (D_MODEL // 2) // NA_HEAD_DIM
NA_DIM = NA_HEADS * NA_HEAD_DIM
NA_KH = 8
NA_KW = 16
N_BRANCH = 3
D_FF = 4 * D_MODEL
Q_OFF = 2 * CONV_DIM + 2 * SGU_DIM
K_OFF = Q_OFF + NA_DIM
V_OFF = K_OFF + NA_DIM
G_OFF = V_OFF + NA_DIM
IN_DIM = G_OFF + N_BRANCH * D_MODEL
NEG_INF = -1e30

kernel_name = 'hybrid_conv_sgu_natten_prefix_block'


def rms_norm(x, g):
    xf = x.astype(jnp.float32)
    y = xf * lax.rsqrt(jnp.mean(xf * xf, axis=-1, keepdims=True) + EPS)
    return (y * g.astype(jnp.float32)).astype(x.dtype)


def layer_norm(x, g, b):
    xf = x.astype(jnp.float32)
    xc = xf - jnp.mean(xf, axis=-1, keepdims=True)
    var = jnp.mean(xc * xc, axis=-1, keepdims=True)
    return (xc * lax.rsqrt(var + EPS) * g.astype(jnp.float32) + b.astype(jnp.float32)).astype(x.dtype)


def modulate(h, shift, scale):
    return h * (1 + scale[:, None, :]) + shift[:, None, :]


def split_heads(t):
    return t.reshape(t.shape[0], t.shape[1], NA_HEADS, NA_HEAD_DIM)


def conv_branch(a, conv_w, conv_b, ln_g, ln_b, w_out):
    a1, a2 = jnp.split(a, 2, axis=-1)
    u = a1 * jax.nn.sigmoid(a2)
    pad = CONV_WIDTH // 2
    u = lax.conv_general_dilated(u, conv_w[:, None, :], window_strides=(1,), padding=[(pad, pad)],
                                 dimension_numbers=('NWC', 'WIO', 'NWC'),
                                 feature_group_count=CONV_DIM) + conv_b
    u = jax.nn.silu(layer_norm(u, ln_g, ln_b))
    return u @ w_out


def sgu_branch(uv, ln_g, ln_b, w_s, b_s, w_out):
    u, v = jnp.split(jax.nn.gelu(uv), 2, axis=-1)
    v = layer_norm(v, ln_g, ln_b)
    bsz, n, _ = v.shape
    v = v.reshape(bsz, n // SGU_CHUNK, SGU_CHUNK, SGU_GROUPS, SGU_DIM // SGU_GROUPS)
    v = jnp.einsum('gpq,bnqgc->bnpgc', w_s, v) + b_s.T[None, None, :, :, None]
    return (u * v.reshape(bsz, n, SGU_DIM)) @ w_out


def neighbourhood_attention(q, k, v, kc, vc, rpb):
    bsz, s, h, dh = q.shape
    rows = s // GRID_W
    kh = min(NA_KH, rows)
    ncb = GRID_W // NA_KW
    kbw = 2 * NA_KW
    r = np.arange(rows)
    row_idx = np.clip(r - kh // 2, 0, rows - kh)[:, None] + np.arange(kh)[None]
    dr = row_idx - r[:, None]
    j = np.arange(ncb)
    col_idx = np.clip(j * NA_KW - NA_KW // 2, 0, GRID_W - kbw)[:, None] + np.arange(kbw)[None]
    qc = j[:, None] * NA_KW + np.arange(NA_KW)[None]
    win_start = np.clip(qc - NA_KW // 2, 0, GRID_W - NA_KW)
    off = col_idx[:, None, :] - win_start[:, :, None]
    col_valid = (off >= 0) & (off < NA_KW)
    dc = np.clip(col_idx[:, None, :] - qc[:, :, None] + NA_KW - 1, 0, 2 * NA_KW - 2)
    bias = rpb[:, (dr + NA_KH - 1)[:, None, None, :, None], dc[None, :, :, None, :]]
    bias = jnp.where(col_valid[None, None, :, :, None, :], bias.astype(jnp.float32), NEG_INF)

    qg = q.reshape(bsz, rows, ncb, NA_KW, h, dh) * NA_HEAD_DIM ** -0.5
    kg = k.reshape(bsz, rows, GRID_W, h, dh)
    vg = v.reshape(bsz, rows, GRID_W, h, dh)
    ri = row_idx[:, None, :, None]
    ci = col_idx[None, :, None, :]
    k_blk = kg[:, ri, ci]
    v_blk = vg[:, ri, ci]
    s_loc = jnp.einsum('brjqhd,brjiwhd->bhrjqiw', qg, k_blk).astype(jnp.float32) + bias[None]
    s_loc = s_loc.reshape(bsz, h, rows, ncb, NA_KW, kh * kbw)
    s_ctx = jnp.einsum('brjqhd,bchd->bhrjqc', qg, kc).astype(jnp.float32)
    p = jax.nn.softmax(jnp.concatenate([s_loc, s_ctx], axis=-1), axis=-1).astype(v.dtype)
    p_loc = p[..., :kh * kbw].reshape(bsz, h, rows, ncb, NA_KW, kh, kbw)
    p_ctx = p[..., kh * kbw:]
    o = (jnp.einsum('bhrjqiw,brjiwhd->brjqhd', p_loc, v_blk)
         + jnp.einsum('bhrjqc,bchd->brjqhd', p_ctx, vc))
    return o.reshape(bsz, s, NA_DIM)


def context_attention(q, k, v):
    s = jnp.einsum('bqhd,bkhd->bhqk', q * NA_HEAD_DIM ** -0.5, k).astype(jnp.float32)
    p = jax.nn.softmax(s, axis=-1).astype(v.dtype)
    o = jnp.einsum('bhqk,bkhd->bqhd', p, v)
    return o.reshape(o.shape[0], o.shape[1], NA_DIM)


def mixer_output(z, y_na, conv_w, conv_b, conv_ln_g, conv_ln_b, w_conv_out, sgu_ln_g, sgu_ln_b,
                 sgu_w, sgu_b, w_sgu_out, w_na_out, w_o, b_o):
    y_conv = conv_branch(z[..., :2 * CONV_DIM], conv_w, conv_b, conv_ln_g, conv_ln_b, w_conv_out)
    y_sgu = sgu_branch(z[..., 2 * CONV_DIM:Q_OFF], sgu_ln_g, sgu_ln_b, sgu_w, sgu_b, w_sgu_out)
    y_att = y_na @ w_na_out
    g_conv, g_sgu, g_att = jnp.split(jax.nn.sigmoid(z[..., G_OFF:]), N_BRANCH, axis=-1)
    return (g_conv * y_conv + g_sgu * y_sgu + g_att * y_att) @ w_o + b_o


def sq_relu_ffn(h, w1, b1, w2, b2):
    return jnp.square(jax.nn.relu(h @ w1 + b1)) @ w2 + b2


def setup_inputs(seed: int = 0) -> dict:
    key = jax.random.key(seed)
    ks = iter(jax.random.split(key, 40))
    f32 = jnp.float32

    def nrm(shape, scale):
        return jax.random.normal(next(ks), shape, f32) * scale

    L = DEPTH
    return {
        'x': nrm((BATCH, SEQ, D_MODEL), 1.0),
        'c': nrm((BATCH, D_MODEL), 1.0),
        'ctx': nrm((BATCH, CTX_LEN, D_MODEL), 1.0),
        'c_ctx': nrm((D_MODEL,), 1.0),
        'w_mod': nrm((L, D_MODEL, 6 * D_MODEL), 0.5 * D_MODEL ** -0.5),
        'b_mod': nrm((L, 6 * D_MODEL), 0.01),
        'norm1_g': 1.0 + nrm((L, D_MODEL), 0.05),
        'norm2_g': 1.0 + nrm((L, D_MODEL), 0.05),
        'w_in': nrm((L, D_MODEL, IN_DIM), D_MODEL ** -0.5),
        'b_in': nrm((L, IN_DIM), 0.01),
        'conv_w': nrm((L, CONV_WIDTH, CONV_DIM), CONV_WIDTH ** -0.5),
        'conv_b': nrm((L, CONV_DIM), 0.01),
        'conv_ln_g': 1.0 + nrm((L, CONV_DIM), 0.05),
        'conv_ln_b': nrm((L, CONV_DIM), 0.01),
        'w_conv_out': nrm((L, CONV_DIM, D_MODEL), CONV_DIM ** -0.5),
        'sgu_ln_g': 1.0 + nrm((L, SGU_DIM), 0.05),
        'sgu_ln_b': nrm((L, SGU_DIM), 0.01),
        'sgu_w': nrm((L, SGU_GROUPS, SGU_CHUNK, SGU_CHUNK), SGU_CHUNK ** -0.5),
        'sgu_b': 1.0 + nrm((L, SGU_GROUPS, SGU_CHUNK), 0.1),
        'w_sgu_out': nrm((L, SGU_DIM, D_MODEL), SGU_DIM ** -0.5),
        'na_rpb': nrm((L, NA_HEADS, 2 * NA_KH - 1, 2 * NA_KW - 1), 0.5),
        'w_na_out': nrm((L, NA_DIM, D_MODEL), NA_DIM ** -0.5),
        'w_o': nrm((L, D_MODEL, D_MODEL), D_MODEL ** -0.5),
        'b_o': nrm((L, D_MODEL), 0.01),
        'w_ff1': nrm((L, D_MODEL, D_FF), D_MODEL ** -0.5),
        'b_ff1': nrm((L, D_FF), 0.01),
        'w_ff2': nrm((L, D_FF, D_MODEL), D_FF ** -0.5),
        'b_ff2': nrm((L, D_MODEL), 0.01),
        'final_g': 1.0 + nrm((D_MODEL,), 0.05),
    }


def reference(x, c, ctx, c_ctx, w_mod, b_mod, norm1_g, norm2_g, w_in, b_in, conv_w, conv_b,
              conv_ln_g, conv_ln_b, w_conv_out, sgu_ln_g, sgu_ln_b, sgu_w, sgu_b, w_sgu_out,
              na_rpb, w_na_out, w_o, b_o, w_ff1, b_ff1, w_ff2, b_ff2, final_g):
    silu_c = jax.nn.silu(c)
    silu_cc = jax.nn.silu(c_ctx)[None]
    xl, xc = x, ctx
    for l in range(DEPTH):
        last = l == DEPTH - 1
        mod_l = jnp.split(silu_c @ w_mod[l] + b_mod[l], 6, axis=-1)
        mod_c = jnp.split(silu_cc @ w_mod[l] + b_mod[l], 6, axis=-1)
        branch_params = (conv_w[l], conv_b[l], conv_ln_g[l], conv_ln_b[l], w_conv_out[l],
                         sgu_ln_g[l], sgu_ln_b[l], sgu_w[l], sgu_b[l], w_sgu_out[l],
                         w_na_out[l], w_o[l], b_o[l])
        hl = modulate(rms_norm(xl, norm1_g[l]), mod_l[0], mod_l[1])
        hc = modulate(rms_norm(xc, norm1_g[l]), mod_c[0], mod_c[1])
        zl = hl @ w_in[l] + b_in[l]
        if last:
            zkv = hc @ w_in[l, :, K_OFF:G_OFF] + b_in[l, K_OFF:G_OFF]
            kc, vc = split_heads(zkv[..., :NA_DIM]), split_heads(zkv[..., NA_DIM:])
        else:
            zc = hc @ w_in[l] + b_in[l]
            kc, vc = split_heads(zc[..., K_OFF:V_OFF]), split_heads(zc[..., V_OFF:G_OFF])
        y_na_l = neighbourhood_attention(split_heads(zl[..., Q_OFF:K_OFF]), split_heads(zl[..., K_OFF:V_OFF]),
                                         split_heads(zl[..., V_OFF:G_OFF]), kc, vc, na_rpb[l])
        xl = xl + mod_l[2][:, None, :] * mixer_output(zl, y_na_l, *branch_params)
        hl2 = modulate(rms_norm(xl, norm2_g[l]), mod_l[3], mod_l[4])
        xl = xl + mod_l[5][:, None, :] * sq_relu_ffn(hl2, w_ff1[l], b_ff1[l], w_ff2[l], b_ff2[l])
        if not last:
            y_na_c = context_attention(split_heads(zc[..., Q_OFF:K_OFF]), kc, vc)
            xc = xc + mod_c[2][:, None, :] * mixer_output(zc, y_na_c, *branch_params)
            hc2 = modulate(rms_norm(xc, norm2_g[l]), mod_c[3], mod_c[4])
            xc = xc + mod_c[5][:, None, :] * sq_relu_ffn(hc2, w_ff1[l], b_ff1[l], w_ff2[l], b_ff2[l])
    return rms_norm(xl, final_g)
```

```python
import functools

import jax
import jax.numpy as jnp
import numpy as np
from jax import lax
from jax.experimental import pallas as pl
from jax.experimental.pallas import tpu as pltpu

F32 = jnp.float32
BF16 = jnp.bfloat16

D_MODEL = 2048
BATCH = 2
SEQ = 4096
DEPTH = 4
GRID_W = 64
GRID_ROWS = SEQ // GRID_W
CTX_LEN = 256
EPS = 1e-6
CONV_DIM = D_MODEL // 2
CONV_WIDTH = 31
CONV_PAD = CONV_WIDTH // 2
SGU_DIM = D_MODEL // 2
SGU_CHUNK = 128
SGU_GROUPS = SGU_DIM // 128
NA_HEAD_DIM = 64
NA_HEADS = (D_MODEL // 2) // NA_HEAD_DIM
NA_DIM = NA_HEADS * NA_HEAD_DIM
NA_KH = 8
NA_KW = 16
D_FF = 4 * D_MODEL
Q_OFF = 2 * CONV_DIM + 2 * SGU_DIM
K_OFF = Q_OFF + NA_DIM
V_OFF = K_OFF + NA_DIM
G_OFF = V_OFF + NA_DIM
IN_DIM = G_OFF + 3 * D_MODEL
NEG_INF = -1e30

T_LAT = BATCH * SEQ
T_CTX = BATCH * CTX_LEN
T_ALL = T_LAT + T_CTX
CTX_BASE = T_LAT
MOD_ROWS = 8
LANES = 128
HALO = 16
VMEM_LIMIT = 56 * 1024 * 1024


def _segment(row0):
    return jnp.minimum(row0 // SEQ, BATCH)


def _sigmoid(x):
    return 1.0 / (1.0 + jnp.exp(-x))


def _params(*sem):
    return pltpu.CompilerParams(dimension_semantics=sem, vmem_limit_bytes=VMEM_LIMIT)


def _mod_kernel(c_ref, w_ref, b_ref, o_ref):
    c = c_ref[...]
    s = c * _sigmoid(c)
    o_ref[...] = jnp.dot(s.astype(BF16), w_ref[...].astype(BF16),
                         preferred_element_type=F32) + b_ref[...]


def _mod_table(c_rows, w_mod, b_mod):
    tn = 1024
    n = 6 * D_MODEL
    return pl.pallas_call(
        _mod_kernel,
        grid=(DEPTH, n // tn),
        in_specs=[pl.BlockSpec((MOD_ROWS, D_MODEL), lambda l, j: (0, 0)),
                  pl.BlockSpec((None, D_MODEL, tn), lambda l, j: (l, 0, j)),
                  pl.BlockSpec((None, 1, tn), lambda l, j: (l, 0, j))],
        out_specs=pl.BlockSpec((None, MOD_ROWS, tn), lambda l, j: (l, 0, j)),
        out_shape=jax.ShapeDtypeStruct((DEPTH, MOD_ROWS, n), F32),
        compiler_params=_params("arbitrary", "arbitrary"),
        name="mod_table",
    )(c_rows, w_mod, b_mod.reshape(DEPTH, 1, n))


def _norm_kernel(x_ref, g_ref, sh_ref, sc_ref, o_ref, *, tm):
    seg = _segment(pl.program_id(0) * tm)
    x = x_ref[...]
    y = x * lax.rsqrt(jnp.mean(x * x, axis=-1, keepdims=True) + EPS) * g_ref[...]
    o_ref[...] = (y * (1.0 + sc_ref[pl.ds(seg, 1), :]) + sh_ref[pl.ds(seg, 1), :]).astype(o_ref.dtype)


def _mod_norm(x, g, shift, scale):
    tm = 512
    return pl.pallas_call(
        functools.partial(_norm_kernel, tm=tm),
        grid=(T_ALL // tm,),
        in_specs=[pl.BlockSpec((tm, D_MODEL), lambda i: (i, 0)),
                  pl.BlockSpec((1, D_MODEL), lambda i: (0, 0)),
                  pl.BlockSpec((MOD_ROWS, D_MODEL), lambda i: (0, 0)),
                  pl.BlockSpec((MOD_ROWS, D_MODEL), lambda i: (0, 0))],
        out_specs=pl.BlockSpec((tm, D_MODEL), lambda i: (i, 0)),
        out_shape=jax.ShapeDtypeStruct((T_ALL, D_MODEL), BF16),
        compiler_params=_params("arbitrary"),
        name="mod_norm",
    )(x, g.reshape(1, D_MODEL), shift, scale)


def _final_norm_kernel(x_ref, g_ref, o_ref):
    x = x_ref[...]
    o_ref[...] = x * lax.rsqrt(jnp.mean(x * x, axis=-1, keepdims=True) + EPS) * g_ref[...]


def _final_norm(x, g):
    tm = 512
    return pl.pallas_call(
        _final_norm_kernel,
        grid=(T_LAT // tm,),
        in_specs=[pl.BlockSpec((tm, D_MODEL), lambda i: (i, 0)),
                  pl.BlockSpec((1, D_MODEL), lambda i: (0, 0))],
        out_specs=pl.BlockSpec((tm, D_MODEL), lambda i: (i, 0)),
        out_shape=jax.ShapeDtypeStruct((T_LAT, D_MODEL), F32),
        compiler_params=_params("arbitrary"),
        name="final_norm",
    )(x, g.reshape(1, D_MODEL))


def _mm_kernel(x_ref, w_ref, b_ref, *rest, tm, act, residual):
    if residual:
        res_ref, gate_ref, o_ref, wb_ref = rest
    else:
        o_ref, wb_ref = rest
    i = pl.program_id(1)

    @pl.when(i == 0)
    def _():
        wb_ref[...] = w_ref[...].astype(BF16)

    acc = jnp.dot(x_ref[...], wb_ref[...], preferred_element_type=F32) + b_ref[...]
    if act == "relu2":
        acc = jnp.square(jnp.maximum(acc, 0.0))
    if residual:
        seg = _segment(i * tm)
        acc = res_ref[...] + gate_ref[pl.ds(seg, 1), :] * acc
    o_ref[...] = acc.astype(o_ref.dtype)


def _matmul(x, w, b, *, tm, tn, out_dtype, act=None, res=None, gate=None, name):
    m, k = x.shape
    n = w.shape[1]
    residual = res is not None
    in_specs = [pl.BlockSpec((tm, k), lambda j, i: (i, 0)),
                pl.BlockSpec((k, tn), lambda j, i: (0, j)),
                pl.BlockSpec((1, tn), lambda j, i: (0, j))]
    args = [x, w, b.reshape(1, n)]
    if residual:
        in_specs += [pl.BlockSpec((tm, tn), lambda j, i: (i, j)),
                     pl.BlockSpec((MOD_ROWS, tn), lambda j, i: (0, j))]
        args += [res, gate]
    return pl.pallas_call(
        functools.partial(_mm_kernel, tm=tm, act=act, residual=residual),
        grid=(n // tn, m // tm),
        in_specs=in_specs,
        out_specs=pl.BlockSpec((tm, tn), lambda j, i: (i, j)),
        out_shape=jax.ShapeDtypeStruct((m, n), out_dtype),
        scratch_shapes=[pltpu.VMEM((k, tn), BF16)],
        compiler_params=_params("arbitrary", "arbitrary"),
        name=name,
    )(*args)


def _conv_kernel(zc_ref, zp_ref, zn_ref, w_ref, cb_ref, lg_ref, lb_ref, o_ref, ubuf, cbuf, *, tm, rc):
    row0 = pl.program_id(0) * tm
    end = row0 + tm
    seq_start = (row0 == 0) | (row0 == SEQ) | (row0 == T_LAT) | (row0 == T_LAT + CTX_LEN)
    seq_end = (end == SEQ) | (end == T_LAT) | (end == T_LAT + CTX_LEN) | (end == T_ALL)

    def glu(z):
        a = z.astype(F32)
        return a[:, :CONV_DIM] * _sigmoid(a[:, CONV_DIM:])

    ubuf[0:HALO, :] = glu(zp_ref[...]) * jnp.where(seq_start, 0.0, 1.0)
    ubuf[HALO:HALO + tm, :] = glu(zc_ref[...])
    ubuf[HALO + tm:HALO + tm + HALO, :] = glu(zn_ref[...]) * jnp.where(seq_end, 0.0, 1.0)

    tap0 = HALO - CONV_PAD
    for lc in range(CONV_DIM // LANES):
        lanes = pl.ds(lc * LANES, LANES)

        for r0 in range(0, tm, rc):
            acc = jnp.zeros((rc, LANES), F32)
            for j in range(CONV_WIDTH):
                acc = acc + w_ref[pl.ds(j, 1), lanes] * ubuf[pl.ds(r0 + tap0 + j, rc), lanes]
            cbuf[pl.ds(r0, rc), lanes] = acc

    y = cbuf[...] + cb_ref[...]
    yc = y - jnp.mean(y, axis=-1, keepdims=True)
    var = jnp.mean(yc * yc, axis=-1, keepdims=True)
    y = yc * lax.rsqrt(var + EPS) * lg_ref[...] + lb_ref[...]
    o_ref[...] = (y * _sigmoid(y)).astype(o_ref.dtype)


def _conv_branch(z, conv_w, conv_b, ln_g, ln_b):
    tm, rc = 256, 64
    hb = tm // HALO
    last = T_ALL // HALO - 1
    wpad = jnp.zeros((32, CONV_DIM), F32).at[:CONV_WIDTH].set(conv_w)
    vec = lambda a: a.reshape(1, CONV_DIM)
    return pl.pallas_call(
        functools.partial(_conv_kernel, tm=tm, rc=rc),
        grid=(T_ALL // tm,),
        in_specs=[pl.BlockSpec((tm, 2 * CONV_DIM), lambda i: (i, 0)),
                  pl.BlockSpec((HALO, 2 * CONV_DIM), lambda i: (jnp.maximum(i * hb - 1, 0), 0)),
                  pl.BlockSpec((HALO, 2 * CONV_DIM), lambda i: (jnp.minimum((i + 1) * hb, last), 0)),
                  pl.BlockSpec((32, CONV_DIM), lambda i: (0, 0)),
                  pl.BlockSpec((1, CONV_DIM), lambda i: (0, 0)),
                  pl.BlockSpec((1, CONV_DIM), lambda i: (0, 0)),
                  pl.BlockSpec((1, CONV_DIM), lambda i: (0, 0))],
        out_specs=pl.BlockSpec((tm, CONV_DIM), lambda i: (i, 0)),
        out_shape=jax.ShapeDtypeStruct((T_ALL, CONV_DIM), BF16),
        scratch_shapes=[pltpu.VMEM((tm + 2 * HALO, CONV_DIM), F32),
                        pltpu.VMEM((tm, CONV_DIM), F32)],
        compiler_params=_params("arbitrary"),
        name="conv_branch",
    )(z, z, z, wpad, vec(conv_b), vec(ln_g), vec(ln_b))


def _gelu_tanh(x):
    return 0.5 * x * (1.0 + jnp.tanh(np.sqrt(2.0 / np.pi).astype(np.float32) * (x + 0.044715 * (x * x * x))))


def _sgu_kernel(z_ref, lg_ref, lb_ref, ws_ref, bs_ref, o_ref, vbuf, *, tm):
    v = _gelu_tanh(z_ref[:, SGU_DIM:].astype(F32))
    vc = v - jnp.mean(v, axis=-1, keepdims=True)
    var = jnp.mean(vc * vc, axis=-1, keepdims=True)
    vbuf[...] = (vc * lax.rsqrt(var + EPS) * lg_ref[...] + lb_ref[...]).astype(BF16)
    for c in range(tm // SGU_CHUNK):
        rows = pl.ds(c * SGU_CHUNK, SGU_CHUNK)
        for gi in range(SGU_GROUPS):
            lanes = pl.ds(gi * LANES, LANES)
            sv = jnp.dot(ws_ref[gi].astype(BF16), vbuf[rows, lanes], preferred_element_type=F32)
            u = _gelu_tanh(z_ref[rows, lanes].astype(F32))
            o_ref[rows, lanes] = (u * (sv + bs_ref[:, lanes])).astype(o_ref.dtype)


def _sgu_branch(z, ln_g, ln_b, w_s, b_s):
    tm = 256
    bs_full = jnp.repeat(b_s.T, SGU_DIM // SGU_GROUPS, axis=1)
    vec = lambda a: a.reshape(1, SGU_DIM)
    return pl.pallas_call(
        functools.partial(_sgu_kernel, tm=tm),
        grid=(T_ALL // tm,),
        in_specs=[pl.BlockSpec((tm, 2 * SGU_DIM), lambda i: (i, 1)),
                  pl.BlockSpec((1, SGU_DIM), lambda i: (0, 0)),
                  pl.BlockSpec((1, SGU_DIM), lambda i: (0, 0)),
                  pl.BlockSpec((SGU_GROUPS, SGU_CHUNK, SGU_CHUNK), lambda i: (0, 0, 0)),
                  pl.BlockSpec((SGU_CHUNK, SGU_DIM), lambda i: (0, 0))],
        out_specs=pl.BlockSpec((tm, SGU_DIM), lambda i: (i, 0)),
        out_shape=jax.ShapeDtypeStruct((T_ALL, SGU_DIM), BF16),
        scratch_shapes=[pltpu.VMEM((tm, SGU_DIM), BF16)],
        compiler_params=_params("arbitrary"),
        name="sgu_branch",
    )(z, vec(ln_g), vec(ln_b), w_s, bs_full)


def _bias_table(rpb):
    qcol = np.arange(GRID_W)[:, None]
    kcol = np.arange(GRID_W)[None, :]
    win = np.clip(qcol - NA_KW // 2, 0, GRID_W - NA_KW)
    valid = (kcol - win >= 0) & (kcol - win < NA_KW)
    dc = np.clip(kcol - qcol + NA_KW - 1, 0, 2 * NA_KW - 2)
    dr = np.arange(NA_KH)[:, None] + np.arange(NA_KH)[None, :]
    t = rpb[:, dr[:, None, :, None], dc[None, :, None, :]]
    t = jnp.where(valid[None, None, :, None, :], t, NEG_INF)
    return t.reshape(NA_HEADS, NA_KH, GRID_W, NA_KH * GRID_W)


def _softmax_pv(s_parts, v_parts):
    m = functools.reduce(jnp.maximum, [jnp.max(s, axis=-1, keepdims=True) for s in s_parts])
    es = [jnp.exp(s - m) for s in s_parts]
    l = functools.reduce(jnp.add, [jnp.sum(e, axis=-1, keepdims=True) for e in es])
    o = functools.reduce(jnp.add, [jnp.dot(e.astype(BF16), v, preferred_element_type=F32)
                                   for e, v in zip(es, v_parts)])
    return o / l


def _nt_dot(a, b):
    return lax.dot_general(a, b, (((1,), (1,)), ((), ())), preferred_element_type=F32)


def _attn_kernel(q_ref, k_ref, v_ref, bias_ref, o_ref):
    lane = lax.broadcasted_iota(jnp.int32, (1, LANES), 1)
    head_mask = [lane < NA_HEAD_DIM, lane >= NA_HEAD_DIM]
    scale = NA_HEAD_DIM ** -0.5
    win_rows = NA_KH * GRID_W

    def heads_out(q, fn):
        outs = []
        for h in range(2):
            qh = jnp.where(head_mask[h], q, jnp.zeros_like(q)) * jnp.asarray(scale, q.dtype)
            outs.append(fn(h, qh))
        return jnp.where(head_mask[0], outs[0], outs[1])

    for b in range(BATCH):
        base = b * SEQ
        cbase = CTX_BASE + b * CTX_LEN
        kc = k_ref[cbase:cbase + CTX_LEN, :]
        vc = v_ref[cbase:cbase + CTX_LEN, :]

        def row_body(r, carry, base=base, kc=kc, vc=vc):
            start = jnp.clip(r - NA_KH // 2, 0, GRID_ROWS - NA_KH)
            d0 = start - r + NA_KH - 1
            qrow = pl.multiple_of(base + r * GRID_W, GRID_W)
            krow = pl.multiple_of(base + start * GRID_W, GRID_W)
            q = q_ref[pl.ds(qrow, GRID_W), :]
            kl = k_ref[pl.ds(krow, win_rows), :]
            vl = v_ref[pl.ds(krow, win_rows), :]

            def one_head(h, qh):
                s_loc = _nt_dot(qh, kl) + bias_ref[h, d0]
                s_ctx = _nt_dot(qh, kc)
                return _softmax_pv([s_loc, s_ctx], [vl, vc])

            o_ref[pl.ds(qrow, GRID_W), :] = heads_out(q, one_head).astype(o_ref.dtype)
            return carry

        lax.fori_loop(0, GRID_ROWS, row_body, 0)

        qc = q_ref[cbase:cbase + CTX_LEN, :]
        o_ref[cbase:cbase + CTX_LEN, :] = heads_out(
            qc, lambda h, qh: _softmax_pv([_nt_dot(qh, kc)], [vc])).astype(o_ref.dtype)


def _attention(z, bias):
    blk = lambda off: pl.BlockSpec((T_ALL, LANES), lambda p, off=off: (0, off // LANES + p))
    return pl.pallas_call(
        _attn_kernel,
        grid=(NA_HEADS // 2,),
        in_specs=[blk(Q_OFF), blk(K_OFF), blk(V_OFF),
                  pl.BlockSpec((2, NA_KH, GRID_W, NA_KH * GRID_W), lambda p: (p, 0, 0, 0))],
        out_specs=pl.BlockSpec((T_ALL, LANES), lambda p: (0, p)),
        out_shape=jax.ShapeDtypeStruct((T_ALL, NA_DIM), BF16),
        compiler_params=_params("arbitrary"),
        name="attention",
    )(z, z, z, bias)


def _merge_kernel(uc_ref, us_ref, ua_ref, wc_ref, ws_ref, wa_ref, gc_ref, gs_ref, ga_ref, o_ref,
                  wcb, wsb, wab):
    @pl.when(pl.program_id(1) == 0)
    def _():
        wcb[...] = wc_ref[...].astype(BF16)
        wsb[...] = ws_ref[...].astype(BF16)
        wab[...] = wa_ref[...].astype(BF16)

    def term(u_ref, wb, g_ref):
        return _sigmoid(g_ref[...].astype(F32)) * jnp.dot(u_ref[...], wb[...], preferred_element_type=F32)

    o_ref[...] = (term(uc_ref, wcb, gc_ref) + term(us_ref, wsb, gs_ref)
                  + term(ua_ref, wab, ga_ref)).astype(o_ref.dtype)


def _merge(z, u_conv, u_sgu, u_att, w_conv_out, w_sgu_out, w_na_out):
    tm, tn = 512, 512
    k = CONV_DIM
    act = pl.BlockSpec((tm, k), lambda j, i: (i, 0))
    wsp = pl.BlockSpec((k, tn), lambda j, i: (0, j))
    gate = lambda g: pl.BlockSpec((tm, tn), lambda j, i, g=g: (i, (G_OFF + g * D_MODEL) // tn + j))
    return pl.pallas_call(
        _merge_kernel,
        grid=(D_MODEL // tn, T_ALL // tm),
        in_specs=[act, act, act, wsp, wsp, wsp, gate(0), gate(1), gate(2)],
        out_specs=pl.BlockSpec((tm, tn), lambda j, i: (i, j)),
        out_shape=jax.ShapeDtypeStruct((T_ALL, D_MODEL), BF16),
        scratch_shapes=[pltpu.VMEM((k, tn), BF16)] * 3,
        compiler_params=_params("arbitrary", "arbitrary"),
        name="merge",
    )(u_conv, u_sgu, u_att, w_conv_out, w_sgu_out, w_na_out, z, z, z)


def kernel(x, c, ctx, c_ctx, w_mod, b_mod, norm1_g, norm2_g, w_in, b_in, conv_w, conv_b, conv_ln_g, conv_ln_b, w_conv_out, sgu_ln_g, sgu_ln_b, sgu_w, sgu_b, w_sgu_out, na_rpb, w_na_out, w_o, b_o, w_ff1, b_ff1, w_ff2, b_ff2, final_g):
    xs = jnp.concatenate([x.reshape(T_LAT, D_MODEL), ctx.reshape(T_CTX, D_MODEL)], axis=0)
    c_rows = jnp.zeros((MOD_ROWS, D_MODEL), F32).at[:BATCH].set(c).at[BATCH].set(c_ctx)
    mods = _mod_table(c_rows, w_mod, b_mod)

    for l in range(DEPTH):
        mod = [mods[l, :, k * D_MODEL:(k + 1) * D_MODEL] for k in range(6)]
        h = _mod_norm(xs, norm1_g[l], mod[0], mod[1])
        z = _matmul(h, w_in[l], b_in[l], tm=512, tn=1024, out_dtype=BF16, name="in_proj")
        u_conv = _conv_branch(z, conv_w[l], conv_b[l], conv_ln_g[l], conv_ln_b[l])
        u_sgu = _sgu_branch(z, sgu_ln_g[l], sgu_ln_b[l], sgu_w[l], sgu_b[l])
        u_att = _attention(z, _bias_table(na_rpb[l]))
        merged = _merge(z, u_conv, u_sgu, u_att, w_conv_out[l], w_sgu_out[l], w_na_out[l])
        xs = _matmul(merged, w_o[l], b_o[l], tm=512, tn=1024, out_dtype=F32, res=xs, gate=mod[2],
                     name="out_proj")
        h2 = _mod_norm(xs, norm2_g[l], mod[3], mod[4])
        a = _matmul(h2, w_ff1[l], b_ff1[l], tm=512, tn=1024, out_dtype=BF16, act="relu2", name="ffn1")
        xs = _matmul(a, w_ff2[l], b_ff2[l], tm=256, tn=512, out_dtype=F32, res=xs, gate=mod[5],
                     name="ffn2")

    return _final_norm(xs, final_g).reshape(BATCH, SEQ, D_MODEL)
```

```python
import functools

import jax
import jax.numpy as jnp
import numpy as np
from jax import lax
from jax.experimental import pallas as pl
from jax.experimental.pallas import tpu as pltpu

F32 = jnp.float32
BF16 = jnp.bfloat16

D_MODEL = 2048
BATCH = 2
SEQ = 4096
DEPTH = 4
GRID_W = 64
GRID_ROWS = SEQ // GRID_W
CTX_LEN = 256
EPS = 1e-6
CONV_DIM = D_MODEL // 2
CONV_WIDTH = 31
CONV_PAD = CONV_WIDTH // 2
SGU_DIM = D_MODEL // 2
SGU_CHUNK = 128
SGU_GROUPS = SGU_DIM // 128
NA_HEAD_DIM = 64
NA_HEADS = (D_MODEL // 2) // NA_HEAD_DIM
NA_DIM = NA_HEADS * NA_HEAD_DIM
NA_KH = 8
NA_KW = 16
RPB_ROWS = 2 * NA_KH - 1
RPB_COLS = 2 * NA_KW - 1
D_FF = 4 * D_MODEL
Q_OFF = 2 * CONV_DIM + 2 * SGU_DIM
K_OFF = Q_OFF + NA_DIM
V_OFF = K_OFF + NA_DIM
G_OFF = V_OFF + NA_DIM
IN_DIM = G_OFF + 3 * D_MODEL
NEG_INF = -1e30

T_LAT = BATCH * SEQ
T_CTX = BATCH * CTX_LEN
T_ALL = T_LAT + T_CTX
CTX_BASE = T_LAT
MOD_ROWS = 8
LANES = 128
HALO = 16
CONV_W_ROWS = 32
VMEM_LIMIT = 56 * 1024 * 1024


def _segment(row0):
    return jnp.minimum(row0 // SEQ, BATCH)


def _sigmoid(x):
    return 1.0 / (1.0 + jnp.exp(-x))


def _params(*sem):
    return pltpu.CompilerParams(dimension_semantics=sem, vmem_limit_bytes=VMEM_LIMIT)


def _layer_vec(stack, l, n):
    nd = lambda *_: (l, 0, 0)
    return pl.BlockSpec((None, 1, n), nd), stack.reshape(DEPTH, 1, n)


def _mod_kernel(c_ref, w_ref, b_ref, o_ref):
    c = c_ref[...]
    s = c * _sigmoid(c)
    o_ref[...] = jnp.dot(s.astype(BF16), w_ref[...].astype(BF16),
                         preferred_element_type=F32) + b_ref[...]


def _mod_table(c_rows, w_mod, b_mod):
    tn = 1024
    n = 6 * D_MODEL
    return pl.pallas_call(
        _mod_kernel,
        grid=(DEPTH, n // tn),
        in_specs=[pl.BlockSpec((MOD_ROWS, D_MODEL), lambda l, j: (0, 0)),
                  pl.BlockSpec((None, D_MODEL, tn), lambda l, j: (l, 0, j)),
                  pl.BlockSpec((None, 1, tn), lambda l, j: (l, 0, j))],
        out_specs=pl.BlockSpec((None, MOD_ROWS, tn), lambda l, j: (l, 0, j)),
        out_shape=jax.ShapeDtypeStruct((DEPTH, MOD_ROWS, n), F32),
        compiler_params=_params("arbitrary", "arbitrary"),
        name="mod_table",
    )(c_rows, w_mod, b_mod.reshape(DEPTH, 1, n))


def _mod_spec(l, k, tn=D_MODEL, col=None):
    per = D_MODEL // tn
    if col is None:
        return pl.BlockSpec((None, MOD_ROWS, tn), lambda *_: (l, 0, k * per))
    return pl.BlockSpec((None, MOD_ROWS, tn), lambda *ids: (l, 0, k * per + col(*ids)))


def _norm_kernel(x_ref, g_ref, sh_ref, sc_ref, o_ref, *, tm):
    seg = _segment(pl.program_id(0) * tm)
    x = x_ref[...]
    y = x * lax.rsqrt(jnp.mean(x * x, axis=-1, keepdims=True) + EPS) * g_ref[...]
    o_ref[...] = (y * (1.0 + sc_ref[pl.ds(seg, 1), :]) + sh_ref[pl.ds(seg, 1), :]).astype(o_ref.dtype)


def _mod_norm(x, g_stack, mods, l, k_shift):
    tm = 512
    g_spec, g_arr = _layer_vec(g_stack, l, D_MODEL)
    return pl.pallas_call(
        functools.partial(_norm_kernel, tm=tm),
        grid=(T_ALL // tm,),
        in_specs=[pl.BlockSpec((tm, D_MODEL), lambda i: (i, 0)), g_spec,
                  _mod_spec(l, k_shift), _mod_spec(l, k_shift + 1)],
        out_specs=pl.BlockSpec((tm, D_MODEL), lambda i: (i, 0)),
        out_shape=jax.ShapeDtypeStruct((T_ALL, D_MODEL), BF16),
        compiler_params=_params("arbitrary"),
        name="mod_norm",
    )(x, g_arr, mods, mods)


def _final_norm_kernel(x_ref, g_ref, o_ref):
    x = x_ref[...]
    o_ref[...] = x * lax.rsqrt(jnp.mean(x * x, axis=-1, keepdims=True) + EPS) * g_ref[...]


def _final_norm(x, g):
    tm = 512
    return pl.pallas_call(
        _final_norm_kernel,
        grid=(T_LAT // tm,),
        in_specs=[pl.BlockSpec((tm, D_MODEL), lambda i: (i, 0)),
                  pl.BlockSpec((1, D_MODEL), lambda i: (0, 0))],
        out_specs=pl.BlockSpec((tm, D_MODEL), lambda i: (i, 0)),
        out_shape=jax.ShapeDtypeStruct((T_LAT, D_MODEL), F32),
        compiler_params=_params("arbitrary"),
        name="final_norm",
    )(x, g.reshape(1, D_MODEL))


def _mm_kernel(x_ref, w_ref, b_ref, *rest, tm, act, residual):
    if residual:
        res_ref, gate_ref, o_ref, wb_ref = rest
    else:
        o_ref, wb_ref = rest
    i = pl.program_id(1)

    @pl.when(i == 0)
    def _():
        wb_ref[...] = w_ref[...].astype(BF16)

    acc = jnp.dot(x_ref[...], wb_ref[...], preferred_element_type=F32) + b_ref[...]
    if act == "relu2":
        acc = jnp.square(jnp.maximum(acc, 0.0))
    if residual:
        seg = _segment(i * tm)
        acc = res_ref[...] + gate_ref[pl.ds(seg, 1), :] * acc
    o_ref[...] = acc.astype(o_ref.dtype)


def _matmul(x, w_stack, b_stack, l, *, tm, tn, out_dtype, act=None, res=None, mods=None, k_gate=None,
            name):
    m, k = x.shape
    n = w_stack.shape[2]
    residual = res is not None
    in_specs = [pl.BlockSpec((tm, k), lambda j, i: (i, 0)),
                pl.BlockSpec((None, k, tn), lambda j, i: (l, 0, j)),
                pl.BlockSpec((None, 1, tn), lambda j, i: (l, 0, j))]
    args = [x, w_stack, b_stack.reshape(DEPTH, 1, n)]
    if residual:
        in_specs += [pl.BlockSpec((tm, tn), lambda j, i: (i, j)),
                     _mod_spec(l, k_gate, tn, col=lambda j, i: j)]
        args += [res, mods]
    return pl.pallas_call(
        functools.partial(_mm_kernel, tm=tm, act=act, residual=residual),
        grid=(n // tn, m // tm),
        in_specs=in_specs,
        out_specs=pl.BlockSpec((tm, tn), lambda j, i: (i, j)),
        out_shape=jax.ShapeDtypeStruct((m, n), out_dtype),
        scratch_shapes=[pltpu.VMEM((k, tn), BF16)],
        compiler_params=_params("arbitrary", "arbitrary"),
        name=name,
    )(*args)


def _conv_kernel(zc_ref, zp_ref, zn_ref, w_ref, cb_ref, lg_ref, lb_ref, o_ref, ubuf, cbuf, *, tm, rc):
    row0 = pl.program_id(0) * tm
    end = row0 + tm
    seq_start = (row0 == 0) | (row0 == SEQ) | (row0 == T_LAT) | (row0 == T_LAT + CTX_LEN)
    seq_end = (end == SEQ) | (end == T_LAT) | (end == T_LAT + CTX_LEN) | (end == T_ALL)

    def glu(z):
        a = z.astype(F32)
        return a[:, :CONV_DIM] * _sigmoid(a[:, CONV_DIM:])

    ubuf[0:HALO, :] = glu(zp_ref[...]) * jnp.where(seq_start, 0.0, 1.0)
    ubuf[HALO:HALO + tm, :] = glu(zc_ref[...])
    ubuf[HALO + tm:HALO + tm + HALO, :] = glu(zn_ref[...]) * jnp.where(seq_end, 0.0, 1.0)

    tap0 = HALO - CONV_PAD
    for lc in range(CONV_DIM // LANES):
        lanes = pl.ds(lc * LANES, LANES)
        for r0 in range(0, tm, rc):
            acc = jnp.zeros((rc, LANES), F32)
            for j in range(CONV_WIDTH):
                acc = acc + w_ref[pl.ds(j, 1), lanes] * ubuf[pl.ds(r0 + tap0 + j, rc), lanes]
            cbuf[pl.ds(r0, rc), lanes] = acc

    y = cbuf[...] + cb_ref[...]
    yc = y - jnp.mean(y, axis=-1, keepdims=True)
    var = jnp.mean(yc * yc, axis=-1, keepdims=True)
    y = yc * lax.rsqrt(var + EPS) * lg_ref[...] + lb_ref[...]
    o_ref[...] = (y * _sigmoid(y)).astype(o_ref.dtype)


def _conv_branch(z, conv_w, conv_b, ln_g, ln_b, l):
    tm, rc = 256, 64
    hb = tm // HALO
    last = T_ALL // HALO - 1
    wpad = jnp.pad(conv_w, ((0, 0), (0, CONV_W_ROWS - CONV_WIDTH), (0, 0)))
    cb_spec, cb = _layer_vec(conv_b, l, CONV_DIM)
    lg_spec, lg = _layer_vec(ln_g, l, CONV_DIM)
    lb_spec, lb = _layer_vec(ln_b, l, CONV_DIM)
    return pl.pallas_call(
        functools.partial(_conv_kernel, tm=tm, rc=rc),
        grid=(T_ALL // tm,),
        in_specs=[pl.BlockSpec((tm, 2 * CONV_DIM), lambda i: (i, 0)),
                  pl.BlockSpec((HALO, 2 * CONV_DIM), lambda i: (jnp.maximum(i * hb - 1, 0), 0)),
                  pl.BlockSpec((HALO, 2 * CONV_DIM), lambda i: (jnp.minimum((i + 1) * hb, last), 0)),
                  pl.BlockSpec((None, CONV_W_ROWS, CONV_DIM), lambda i: (l, 0, 0)),
                  cb_spec, lg_spec, lb_spec],
        out_specs=pl.BlockSpec((tm, CONV_DIM), lambda i: (i, 0)),
        out_shape=jax.ShapeDtypeStruct((T_ALL, CONV_DIM), BF16),
        scratch_shapes=[pltpu.VMEM((tm + 2 * HALO, CONV_DIM), F32),
                        pltpu.VMEM((tm, CONV_DIM), F32)],
        compiler_params=_params("arbitrary"),
        name="conv_branch",
    )(z, z, z, wpad, cb, lg, lb)


def _gelu_tanh(x):
    return 0.5 * x * (1.0 + jnp.tanh(np.sqrt(2.0 / np.pi).astype(np.float32) * (x + 0.044715 * (x * x * x))))


def _sgu_kernel(z_ref, lg_ref, lb_ref, ws_ref, bs_ref, o_ref, vbuf, *, tm):
    v = _gelu_tanh(z_ref[:, SGU_DIM:].astype(F32))
    vc = v - jnp.mean(v, axis=-1, keepdims=True)
    var = jnp.mean(vc * vc, axis=-1, keepdims=True)
    vbuf[...] = (vc * lax.rsqrt(var + EPS) * lg_ref[...] + lb_ref[...]).astype(BF16)
    for c in range(tm // SGU_CHUNK):
        rows = pl.ds(c * SGU_CHUNK, SGU_CHUNK)
        for gi in range(SGU_GROUPS):
            lanes = pl.ds(gi * LANES, LANES)
            sv = jnp.dot(ws_ref[gi].astype(BF16), vbuf[rows, lanes], preferred_element_type=F32)
            u = _gelu_tanh(z_ref[rows, lanes].astype(F32))
            o_ref[rows, lanes] = (u * (sv + bs_ref[:, lanes])).astype(o_ref.dtype)


def _sgu_branch(z, ln_g, ln_b, w_s, b_s, l):
    tm = 256
    bs_full = jnp.repeat(jnp.swapaxes(b_s, 1, 2), SGU_DIM // SGU_GROUPS, axis=2)
    lg_spec, lg = _layer_vec(ln_g, l, SGU_DIM)
    lb_spec, lb = _layer_vec(ln_b, l, SGU_DIM)
    return pl.pallas_call(
        functools.partial(_sgu_kernel, tm=tm),
        grid=(T_ALL // tm,),
        in_specs=[pl.BlockSpec((tm, 2 * SGU_DIM), lambda i: (i, 1)), lg_spec, lb_spec,
                  pl.BlockSpec((None, SGU_GROUPS, SGU_CHUNK, SGU_CHUNK), lambda i: (l, 0, 0, 0)),
                  pl.BlockSpec((None, SGU_CHUNK, SGU_DIM), lambda i: (l, 0, 0))],
        out_specs=pl.BlockSpec((tm, SGU_DIM), lambda i: (i, 0)),
        out_shape=jax.ShapeDtypeStruct((T_ALL, SGU_DIM), BF16),
        scratch_shapes=[pltpu.VMEM((tm, SGU_DIM), BF16)],
        compiler_params=_params("arbitrary"),
        name="sgu_branch",
    )(z, lg, lb, w_s, bs_full)


BIAS_PAIRS = 2 * NA_KH - 2


def _nt_dot(a, b):
    return lax.dot_general(a, b, (((1,), (1,)), ((), ())), preferred_element_type=F32)


def _attn_kernel(rpb_ref, q_ref, k_ref, v_ref, o_ref, bias_s, vaug_s, *, layer):
    pair = pl.program_id(0)
    lane_row = lax.broadcasted_iota(jnp.int32, (1, LANES), 1)
    lo = lane_row < NA_HEAD_DIM
    scale = NA_HEAD_DIM ** -0.5
    win_rows = NA_KH * GRID_W

    qi = lax.broadcasted_iota(jnp.int32, (GRID_W, LANES), 0)
    kcol = lax.broadcasted_iota(jnp.int32, (GRID_W, LANES), 1) & (GRID_W - 1)
    diff = kcol - qi + (NA_KW - 1)
    win = jnp.clip(qi - NA_KW // 2, 0, GRID_W - NA_KW)
    valid = (kcol >= win) & (kcol < win + NA_KW)

    def build(m, carry):
        for h in range(2):
            base = ((layer * NA_HEADS + 2 * pair + h) * RPB_ROWS + m) * RPB_COLS
            val = jnp.zeros((GRID_W, LANES), F32)
            for d in range(RPB_COLS):
                both = jnp.where(lo, rpb_ref[base + d], rpb_ref[base + RPB_COLS + d])
                val = jnp.where(diff == d, both, val)
            bias_s[m, h * GRID_W:(h + 1) * GRID_W, :] = jnp.where(valid, val, NEG_INF)
        return carry

    lax.fori_loop(0, BIAS_PAIRS, build, 0)

    vaug_s[:, :LANES] = v_ref[...]
    vaug_s[:, LANES:] = jnp.ones((T_ALL, LANES), BF16)

    def stack_heads(q):
        q = q * jnp.asarray(scale, q.dtype)
        zero = jnp.zeros_like(q)
        return jnp.concatenate([jnp.where(lo, q, zero), jnp.where(lo, zero, q)], axis=0)

    def softmax_pv(s_parts, v_parts, nq):
        tiles = [s[:, t * LANES:(t + 1) * LANES] for s in s_parts for t in range(s.shape[1] // LANES)]
        m = jnp.max(functools.reduce(jnp.maximum, tiles), axis=-1, keepdims=True)
        o2 = functools.reduce(jnp.add, [
            jnp.dot(jnp.exp(s - m).astype(BF16), v, preferred_element_type=F32)
            for s, v in zip(s_parts, v_parts)])
        o = o2[:, :LANES] / o2[:, LANES:]
        return jnp.where(lo, o[:nq], o[nq:])

    def latent_rows(b, r):
        cbase = CTX_BASE + b * CTX_LEN
        start = jnp.clip(r - NA_KH // 2, 0, GRID_ROWS - NA_KH)
        d0 = start - r + NA_KH - 1
        qrow = pl.multiple_of(b * SEQ + r * GRID_W, GRID_W)
        krow = pl.multiple_of(b * SEQ + start * GRID_W, GRID_W)
        q2 = stack_heads(q_ref[pl.ds(qrow, GRID_W), :])
        bias = jnp.concatenate([bias_s[d0 + 2 * t] for t in range(NA_KH // 2)], axis=1)
        s_loc = _nt_dot(q2, k_ref[pl.ds(krow, win_rows), :]) + bias
        s_ctx = _nt_dot(q2, k_ref[cbase:cbase + CTX_LEN, :])
        o = softmax_pv([s_loc, s_ctx],
                       [vaug_s[pl.ds(krow, win_rows), :], vaug_s[cbase:cbase + CTX_LEN, :]], GRID_W)
        o_ref[pl.ds(qrow, GRID_W), :] = o.astype(o_ref.dtype)

    def row_body(r, carry):
        for b in range(BATCH):
            latent_rows(b, r)
        return carry

    lax.fori_loop(0, GRID_ROWS, row_body, 0, unroll=2)

    for b in range(BATCH):
        cbase = CTX_BASE + b * CTX_LEN
        rows = slice(cbase, cbase + CTX_LEN)
        q2 = stack_heads(q_ref[rows, :])
        o = softmax_pv([_nt_dot(q2, k_ref[rows, :])], [vaug_s[rows, :]], CTX_LEN)
        o_ref[rows, :] = o.astype(o_ref.dtype)


def _attention(z, rpb_flat, l):
    blk = lambda off: pl.BlockSpec((T_ALL, LANES), lambda p, off=off: (0, off // LANES + p))
    return pl.pallas_call(
        functools.partial(_attn_kernel, layer=l),
        grid=(NA_HEADS // 2,),
        in_specs=[pl.BlockSpec(memory_space=pltpu.SMEM), blk(Q_OFF), blk(K_OFF), blk(V_OFF)],
        out_specs=pl.BlockSpec((T_ALL, LANES), lambda p: (0, p)),
        out_shape=jax.ShapeDtypeStruct((T_ALL, NA_DIM), BF16),
        scratch_shapes=[pltpu.VMEM((BIAS_PAIRS, 2 * GRID_W, LANES), F32),
                        pltpu.VMEM((T_ALL, 2 * LANES), BF16)],
        compiler_params=_params("arbitrary"),
        name="attention",
    )(rpb_flat, z, z, z)


def _merge_kernel(uc_ref, us_ref, ua_ref, wc_ref, ws_ref, wa_ref, gc_ref, gs_ref, ga_ref, o_ref,
                  wcb, wsb, wab):
    @pl.when(pl.program_id(1) == 0)
    def _():
        wcb[...] = wc_ref[...].astype(BF16)
        wsb[...] = ws_ref[...].astype(BF16)
        wab[...] = wa_ref[...].astype(BF16)

    def term(u_ref, wb, g_ref):
        return _sigmoid(g_ref[...].astype(F32)) * jnp.dot(u_ref[...], wb[...], preferred_element_type=F32)

    o_ref[...] = (term(uc_ref, wcb, gc_ref) + term(us_ref, wsb, gs_ref)
                  + term(ua_ref, wab, ga_ref)).astype(o_ref.dtype)


def _merge(z, u_conv, u_sgu, u_att, w_conv_out, w_sgu_out, w_na_out, l):
    tm, tn = 512, 512
    k = CONV_DIM
    act = pl.BlockSpec((tm, k), lambda j, i: (i, 0))
    wsp = pl.BlockSpec((None, k, tn), lambda j, i: (l, 0, j))
    gate = lambda g: pl.BlockSpec((tm, tn), lambda j, i, g=g: (i, (G_OFF + g * D_MODEL) // tn + j))
    return pl.pallas_call(
        _merge_kernel,
        grid=(D_MODEL // tn, T_ALL // tm),
        in_specs=[act, act, act, wsp, wsp, wsp, gate(0), gate(1), gate(2)],
        out_specs=pl.BlockSpec((tm, tn), lambda j, i: (i, j)),
        out_shape=jax.ShapeDtypeStruct((T_ALL, D_MODEL), BF16),
        scratch_shapes=[pltpu.VMEM((k, tn), BF16)] * 3,
        compiler_params=_params("arbitrary", "arbitrary"),
        name="merge",
    )(u_conv, u_sgu, u_att, w_conv_out, w_sgu_out, w_na_out, z, z, z)


def kernel(x, c, ctx, c_ctx, w_mod, b_mod, norm1_g, norm2_g, w_in, b_in, conv_w, conv_b, conv_ln_g, conv_ln_b, w_conv_out, sgu_ln_g, sgu_ln_b, sgu_w, sgu_b, w_sgu_out, na_rpb, w_na_out, w_o, b_o, w_ff1, b_ff1, w_ff2, b_ff2, final_g):
    xs = jnp.concatenate([x.reshape(T_LAT, D_MODEL), ctx.reshape(T_CTX, D_MODEL)], axis=0)
    c_rows = jnp.zeros((MOD_ROWS, D_MODEL), F32).at[:BATCH].set(c).at[BATCH].set(c_ctx)
    mods = _mod_table(c_rows, w_mod, b_mod)
    rpb_flat = na_rpb.reshape(-1)

    for l in range(DEPTH):
        h = _mod_norm(xs, norm1_g, mods, l, 0)
        z = _matmul(h, w_in, b_in, l, tm=T_ALL // 8, tn=1024, out_dtype=BF16, name="in_proj")
        u_conv = _conv_branch(z, conv_w, conv_b, conv_ln_g, conv_ln_b, l)
        u_sgu = _sgu_branch(z, sgu_ln_g, sgu_ln_b, sgu_w, sgu_b, l)
        u_att = _attention(z, rpb_flat, l)
        merged = _merge(z, u_conv, u_sgu, u_att, w_conv_out, w_sgu_out, w_na_out, l)
        xs = _matmul(merged, w_o, b_o, l, tm=512, tn=1024, out_dtype=F32, res=xs, mods=mods, k_gate=2,
                     name="out_proj")
        h2 = _mod_norm(xs, norm2_g, mods, l, 3)
        a = _matmul(h2, w_ff1, b_ff1, l, tm=T_ALL // 8, tn=1024, out_dtype=BF16, act="relu2", name="ffn1")
        xs = _matmul(a, w_ff2, b_ff2, l, tm=256, tn=512, out_dtype=F32, res=xs, mods=mods, k_gate=5,
                     name="ffn2")

    return _final_norm(xs, final_g).reshape(BATCH, SEQ, D_MODEL)
```

```python
import functools

import jax
import jax.numpy as jnp
import numpy as np
from jax import lax
from jax.experimental import pallas as pl
from jax.experimental.pallas import tpu as pltpu

F32 = jnp.float32
BF16 = jnp.bfloat16

D_MODEL = 2048
BATCH = 2
SEQ = 4096
DEPTH = 4
GRID_W = 64
GRID_ROWS = SEQ // GRID_W
CTX_LEN = 256
EPS = 1e-6
CONV_DIM = D_MODEL // 2
CONV_WIDTH = 31
CONV_PAD = CONV_WIDTH // 2
SGU_DIM = D_MODEL // 2
SGU_CHUNK = 128
SGU_GROUPS = SGU_DIM // 128
NA_HEAD_DIM = 64
NA_HEADS = (D_MODEL // 2) // NA_HEAD_DIM
NA_DIM = NA_HEADS * NA_HEAD_DIM
NA_KH = 8
NA_KW = 16
RPB_ROWS = 2 * NA_KH - 1
RPB_COLS = 2 * NA_KW - 1
D_FF = 4 * D_MODEL
Q_OFF = 2 * CONV_DIM + 2 * SGU_DIM
K_OFF = Q_OFF + NA_DIM
V_OFF = K_OFF + NA_DIM
G_OFF = V_OFF + NA_DIM
IN_DIM = G_OFF + 3 * D_MODEL
NEG_INF = -1e30

T_LAT = BATCH * SEQ
T_CTX = BATCH * CTX_LEN
T_ALL = T_LAT + T_CTX
CTX_BASE = T_LAT
MOD_ROWS = 8
LANES = 128
HALO = 16
CONV_W_ROWS = 32
VMEM_LIMIT = 56 * 1024 * 1024


def _segment(row0):
    return jnp.minimum(row0 // SEQ, BATCH)


def _sigmoid(x):
    return 1.0 / (1.0 + jnp.exp(-x))


def _params(*sem):
    return pltpu.CompilerParams(dimension_semantics=sem, vmem_limit_bytes=VMEM_LIMIT)


def _layer_vec(stack, l, n):
    nd = lambda *_: (l, 0, 0)
    return pl.BlockSpec((None, 1, n), nd), stack.reshape(DEPTH, 1, n)


def _mod_kernel(c_ref, w_ref, b_ref, o_ref):
    c = c_ref[...]
    s = c * _sigmoid(c)
    o_ref[...] = jnp.dot(s.astype(BF16), w_ref[...].astype(BF16),
                         preferred_element_type=F32) + b_ref[...]


def _mod_table(c_rows, w_mod, b_mod):
    tn = 1024
    n = 6 * D_MODEL
    return pl.pallas_call(
        _mod_kernel,
        grid=(DEPTH, n // tn),
        in_specs=[pl.BlockSpec((MOD_ROWS, D_MODEL), lambda l, j: (0, 0)),
                  pl.BlockSpec((None, D_MODEL, tn), lambda l, j: (l, 0, j)),
                  pl.BlockSpec((None, 1, tn), lambda l, j: (l, 0, j))],
        out_specs=pl.BlockSpec((None, MOD_ROWS, tn), lambda l, j: (l, 0, j)),
        out_shape=jax.ShapeDtypeStruct((DEPTH, MOD_ROWS, n), F32),
        compiler_params=_params("arbitrary", "arbitrary"),
        name="mod_table",
    )(c_rows, w_mod, b_mod.reshape(DEPTH, 1, n))


def _mod_spec(l, k, tn=D_MODEL, col=None):
    per = D_MODEL // tn
    if col is None:
        return pl.BlockSpec((None, MOD_ROWS, tn), lambda *_: (l, 0, k * per))
    return pl.BlockSpec((None, MOD_ROWS, tn), lambda *ids: (l, 0, k * per + col(*ids)))


def _norm_kernel(x_ref, g_ref, sh_ref, sc_ref, o_ref, *, tm):
    seg = _segment(pl.program_id(0) * tm)
    x = x_ref[...]
    y = x * lax.rsqrt(jnp.mean(x * x, axis=-1, keepdims=True) + EPS) * g_ref[...]
    o_ref[...] = (y * (1.0 + sc_ref[pl.ds(seg, 1), :]) + sh_ref[pl.ds(seg, 1), :]).astype(o_ref.dtype)


def _mod_norm(x, g_stack, mods, l, k_shift, rows):
    tm = 512
    g_spec, g_arr = _layer_vec(g_stack, l, D_MODEL)
    return pl.pallas_call(
        functools.partial(_norm_kernel, tm=tm),
        grid=(rows // tm,),
        in_specs=[pl.BlockSpec((tm, D_MODEL), lambda i: (i, 0)), g_spec,
                  _mod_spec(l, k_shift), _mod_spec(l, k_shift + 1)],
        out_specs=pl.BlockSpec((tm, D_MODEL), lambda i: (i, 0)),
        out_shape=jax.ShapeDtypeStruct((rows, D_MODEL), BF16),
        compiler_params=_params("arbitrary"),
        name="mod_norm",
    )(x, g_arr, mods, mods)


def _final_norm_kernel(x_ref, g_ref, o_ref):
    x = x_ref[...]
    o_ref[...] = x * lax.rsqrt(jnp.mean(x * x, axis=-1, keepdims=True) + EPS) * g_ref[...]


def _final_norm(x, g):
    tm = 512
    return pl.pallas_call(
        _final_norm_kernel,
        grid=(T_LAT // tm,),
        in_specs=[pl.BlockSpec((tm, D_MODEL), lambda i: (i, 0)),
                  pl.BlockSpec((1, D_MODEL), lambda i: (0, 0))],
        out_specs=pl.BlockSpec((tm, D_MODEL), lambda i: (i, 0)),
        out_shape=jax.ShapeDtypeStruct((T_LAT, D_MODEL), F32),
        compiler_params=_params("arbitrary"),
        name="final_norm",
    )(x, g.reshape(1, D_MODEL))


def _mm_kernel(x_ref, w_ref, b_ref, *rest, tm, act, residual):
    if residual:
        res_ref, gate_ref, o_ref, wb_ref = rest
    else:
        o_ref, wb_ref = rest
    i = pl.program_id(1)

    @pl.when(i == 0)
    def _():
        wb_ref[...] = w_ref[...].astype(BF16)

    acc = jnp.dot(x_ref[...], wb_ref[...], preferred_element_type=F32) + b_ref[...]
    if act == "relu2":
        acc = jnp.square(jnp.maximum(acc, 0.0))
    if residual:
        seg = _segment(i * tm)
        acc = res_ref[...] + gate_ref[pl.ds(seg, 1), :] * acc
    o_ref[...] = acc.astype(o_ref.dtype)


def _matmul(x, w_stack, b_stack, l, *, tm, tn, out_dtype, act=None, res=None, mods=None, k_gate=None,
            name):
    m, k = x.shape
    n = w_stack.shape[2]
    residual = res is not None
    in_specs = [pl.BlockSpec((tm, k), lambda j, i: (i, 0)),
                pl.BlockSpec((None, k, tn), lambda j, i: (l, 0, j)),
                pl.BlockSpec((None, 1, tn), lambda j, i: (l, 0, j))]
    args = [x, w_stack, b_stack.reshape(DEPTH, 1, n)]
    if residual:
        in_specs += [pl.BlockSpec((tm, tn), lambda j, i: (i, j)),
                     _mod_spec(l, k_gate, tn, col=lambda j, i: j)]
        args += [res, mods]
    return pl.pallas_call(
        functools.partial(_mm_kernel, tm=tm, act=act, residual=residual),
        grid=(n // tn, m // tm),
        in_specs=in_specs,
        out_specs=pl.BlockSpec((tm, tn), lambda j, i: (i, j)),
        out_shape=jax.ShapeDtypeStruct((m, n), out_dtype),
        scratch_shapes=[pltpu.VMEM((k, tn), BF16)],
        compiler_params=_params("arbitrary", "arbitrary"),
        name=name,
    )(*args)


def _conv_kernel(zc_ref, zp_ref, zn_ref, w_ref, cb_ref, lg_ref, lb_ref, o_ref, ubuf, cbuf, *, tm, rc):
    row0 = pl.program_id(0) * tm
    end = row0 + tm
    seq_start = (row0 == 0) | (row0 == SEQ) | (row0 == T_LAT) | (row0 == T_LAT + CTX_LEN)
    seq_end = (end == SEQ) | (end == T_LAT) | (end == T_LAT + CTX_LEN) | (end == T_ALL)

    def glu(z):
        a = z.astype(F32)
        return a[:, :CONV_DIM] * _sigmoid(a[:, CONV_DIM:])

    ubuf[0:HALO, :] = glu(zp_ref[...]) * jnp.where(seq_start, 0.0, 1.0)
    ubuf[HALO:HALO + tm, :] = glu(zc_ref[...])
    ubuf[HALO + tm:HALO + tm + HALO, :] = glu(zn_ref[...]) * jnp.where(seq_end, 0.0, 1.0)

    tap0 = HALO - CONV_PAD
    sub = 8
    for lc in range(CONV_DIM // LANES):
        lanes = pl.ds(lc * LANES, LANES)
        for r0 in range(0, tm, rc):
            out = None
            for rem in range(sub):
                acc = None
                for s in range(rem, tap0 + CONV_WIDTH, sub):
                    if s < tap0:
                        continue
                    term = w_ref[pl.ds(s - tap0, 1), lanes] * ubuf[pl.ds(r0 + s - rem, rc + sub), lanes]
                    acc = term if acc is None else acc + term
                shifted = acc[rem:rem + rc]
                out = shifted if out is None else out + shifted
            cbuf[pl.ds(r0, rc), lanes] = out

    y = cbuf[...] + cb_ref[...]
    yc = y - jnp.mean(y, axis=-1, keepdims=True)
    var = jnp.mean(yc * yc, axis=-1, keepdims=True)
    y = yc * lax.rsqrt(var + EPS) * lg_ref[...] + lb_ref[...]
    o_ref[...] = (y * _sigmoid(y)).astype(o_ref.dtype)


def _conv_branch(z, conv_w, conv_b, ln_g, ln_b, l, rows):
    tm, rc = 256, 64
    hb = tm // HALO
    last = T_ALL // HALO - 1
    wpad = jnp.pad(conv_w, ((0, 0), (0, CONV_W_ROWS - CONV_WIDTH), (0, 0)))
    cb_spec, cb = _layer_vec(conv_b, l, CONV_DIM)
    lg_spec, lg = _layer_vec(ln_g, l, CONV_DIM)
    lb_spec, lb = _layer_vec(ln_b, l, CONV_DIM)
    return pl.pallas_call(
        functools.partial(_conv_kernel, tm=tm, rc=rc),
        grid=(rows // tm,),
        in_specs=[pl.BlockSpec((tm, 2 * CONV_DIM), lambda i: (i, 0)),
                  pl.BlockSpec((HALO, 2 * CONV_DIM), lambda i: (jnp.maximum(i * hb - 1, 0), 0)),
                  pl.BlockSpec((HALO, 2 * CONV_DIM), lambda i: (jnp.minimum((i + 1) * hb, last), 0)),
                  pl.BlockSpec((None, CONV_W_ROWS, CONV_DIM), lambda i: (l, 0, 0)),
                  cb_spec, lg_spec, lb_spec],
        out_specs=pl.BlockSpec((tm, CONV_DIM), lambda i: (i, 0)),
        out_shape=jax.ShapeDtypeStruct((rows, CONV_DIM), BF16),
        scratch_shapes=[pltpu.VMEM((tm + 2 * HALO, CONV_DIM), F32),
                        pltpu.VMEM((tm, CONV_DIM), F32)],
        compiler_params=_params("arbitrary"),
        name="conv_branch",
    )(z, z, z, wpad, cb, lg, lb)


def _gelu_tanh(x):
    return 0.5 * x * (1.0 + jnp.tanh(np.sqrt(2.0 / np.pi).astype(np.float32) * (x + 0.044715 * (x * x * x))))


def _sgu_kernel(z_ref, lg_ref, lb_ref, ws_ref, bs_ref, o_ref, vbuf, *, tm):
    v = _gelu_tanh(z_ref[:, SGU_DIM:].astype(F32))
    vc = v - jnp.mean(v, axis=-1, keepdims=True)
    var = jnp.mean(vc * vc, axis=-1, keepdims=True)
    vbuf[...] = (vc * lax.rsqrt(var + EPS) * lg_ref[...] + lb_ref[...]).astype(BF16)
    for c in range(tm // SGU_CHUNK):
        rows = pl.ds(c * SGU_CHUNK, SGU_CHUNK)
        for gi in range(SGU_GROUPS):
            lanes = pl.ds(gi * LANES, LANES)
            sv = jnp.dot(ws_ref[gi].astype(BF16), vbuf[rows, lanes], preferred_element_type=F32)
            u = _gelu_tanh(z_ref[rows, lanes].astype(F32))
            o_ref[rows, lanes] = (u * (sv + bs_ref[:, lanes])).astype(o_ref.dtype)


def _sgu_branch(z, ln_g, ln_b, w_s, b_s, l, rows):
    tm = 256
    bs_full = jnp.repeat(jnp.swapaxes(b_s, 1, 2), SGU_DIM // SGU_GROUPS, axis=2)
    lg_spec, lg = _layer_vec(ln_g, l, SGU_DIM)
    lb_spec, lb = _layer_vec(ln_b, l, SGU_DIM)
    return pl.pallas_call(
        functools.partial(_sgu_kernel, tm=tm),
        grid=(rows // tm,),
        in_specs=[pl.BlockSpec((tm, 2 * SGU_DIM), lambda i: (i, 1)), lg_spec, lb_spec,
                  pl.BlockSpec((None, SGU_GROUPS, SGU_CHUNK, SGU_CHUNK), lambda i: (l, 0, 0, 0)),
                  pl.BlockSpec((None, SGU_CHUNK, SGU_DIM), lambda i: (l, 0, 0))],
        out_specs=pl.BlockSpec((tm, SGU_DIM), lambda i: (i, 0)),
        out_shape=jax.ShapeDtypeStruct((rows, SGU_DIM), BF16),
        scratch_shapes=[pltpu.VMEM((tm, SGU_DIM), BF16)],
        compiler_params=_params("arbitrary"),
        name="sgu_branch",
    )(z, lg, lb, w_s, bs_full)


BIAS_PAIRS = 2 * NA_KH - 2


def _nt_dot(a, b):
    return lax.dot_general(a, b, (((1,), (1,)), ((), ())), preferred_element_type=F32)


def _attn_kernel(rpb_ref, q_ref, k_ref, v_ref, o_ref, bias_s, vaug_s, *, layer, ctx_queries):
    pair = pl.program_id(0)
    lane_row = lax.broadcasted_iota(jnp.int32, (1, LANES), 1)
    lo = lane_row < NA_HEAD_DIM
    scale = NA_HEAD_DIM ** -0.5
    win_rows = NA_KH * GRID_W

    qi = lax.broadcasted_iota(jnp.int32, (GRID_W, LANES), 0)
    kcol = lax.broadcasted_iota(jnp.int32, (GRID_W, LANES), 1) & (GRID_W - 1)
    diff = kcol - qi + (NA_KW - 1)
    win = jnp.clip(qi - NA_KW // 2, 0, GRID_W - NA_KW)
    valid = (kcol >= win) & (kcol < win + NA_KW)

    def build(m, carry):
        for h in range(2):
            base = ((layer * NA_HEADS + 2 * pair + h) * RPB_ROWS + m) * RPB_COLS
            val = jnp.zeros((GRID_W, LANES), F32)
            for d in range(RPB_COLS):
                both = jnp.where(lo, rpb_ref[base + d], rpb_ref[base + RPB_COLS + d])
                val = jnp.where(diff == d, both, val)
            bias_s[m, h * GRID_W:(h + 1) * GRID_W, :] = jnp.where(valid, val, NEG_INF)
        return carry

    lax.fori_loop(0, BIAS_PAIRS, build, 0)

    vaug_s[:, :LANES] = v_ref[...]
    vaug_s[:, LANES:] = jnp.ones((T_ALL, LANES), BF16)

    def stack_heads(q):
        q = q * jnp.asarray(scale, q.dtype)
        zero = jnp.zeros_like(q)
        return jnp.concatenate([jnp.where(lo, q, zero), jnp.where(lo, zero, q)], axis=0)

    def softmax_pv(s_parts, v_parts, nq):
        tiles = [s[:, t * LANES:(t + 1) * LANES] for s in s_parts for t in range(s.shape[1] // LANES)]
        m = jnp.max(functools.reduce(jnp.maximum, tiles), axis=-1, keepdims=True)
        o2 = functools.reduce(jnp.add, [
            jnp.dot(jnp.exp(s - m).astype(BF16), v, preferred_element_type=F32)
            for s, v in zip(s_parts, v_parts)])
        o = o2[:, :LANES] / o2[:, LANES:]
        return jnp.where(lo, o[:nq], o[nq:])

    def latent_rows(b, r):
        cbase = CTX_BASE + b * CTX_LEN
        start = jnp.clip(r - NA_KH // 2, 0, GRID_ROWS - NA_KH)
        d0 = start - r + NA_KH - 1
        qrow = pl.multiple_of(b * SEQ + r * GRID_W, GRID_W)
        krow = pl.multiple_of(b * SEQ + start * GRID_W, GRID_W)
        q2 = stack_heads(q_ref[pl.ds(qrow, GRID_W), :])
        bias = jnp.concatenate([bias_s[d0 + 2 * t] for t in range(NA_KH // 2)], axis=1)
        s_loc = _nt_dot(q2, k_ref[pl.ds(krow, win_rows), :]) + bias
        s_ctx = _nt_dot(q2, k_ref[cbase:cbase + CTX_LEN, :])
        o = softmax_pv([s_loc, s_ctx],
                       [vaug_s[pl.ds(krow, win_rows), :], vaug_s[cbase:cbase + CTX_LEN, :]], GRID_W)
        o_ref[pl.ds(qrow, GRID_W), :] = o.astype(o_ref.dtype)

    def row_body(r, carry):
        for b in range(BATCH):
            latent_rows(b, r)
        return carry

    lax.fori_loop(0, GRID_ROWS, row_body, 0, unroll=8)

    for b in range(BATCH if ctx_queries else 0):
        cbase = CTX_BASE + b * CTX_LEN
        rows = slice(cbase, cbase + CTX_LEN)
        q2 = stack_heads(q_ref[rows, :])
        o = softmax_pv([_nt_dot(q2, k_ref[rows, :])], [vaug_s[rows, :]], CTX_LEN)
        o_ref[rows, :] = o.astype(o_ref.dtype)


def _attention(z, rpb_flat, l, rows):
    blk = lambda off: pl.BlockSpec((T_ALL, LANES), lambda p, off=off: (0, off // LANES + p))
    return pl.pallas_call(
        functools.partial(_attn_kernel, layer=l, ctx_queries=rows > T_LAT),
        grid=(NA_HEADS // 2,),
        in_specs=[pl.BlockSpec(memory_space=pltpu.SMEM), blk(Q_OFF), blk(K_OFF), blk(V_OFF)],
        out_specs=pl.BlockSpec((rows, LANES), lambda p: (0, p)),
        out_shape=jax.ShapeDtypeStruct((rows, NA_DIM), BF16),
        scratch_shapes=[pltpu.VMEM((BIAS_PAIRS, 2 * GRID_W, LANES), F32),
                        pltpu.VMEM((T_ALL, 2 * LANES), BF16)],
        compiler_params=_params("arbitrary"),
        name="attention",
    )(rpb_flat, z, z, z)


def _merge_kernel(uc_ref, us_ref, ua_ref, wc_ref, ws_ref, wa_ref, gc_ref, gs_ref, ga_ref, o_ref,
                  wcb, wsb, wab):
    @pl.when(pl.program_id(1) == 0)
    def _():
        wcb[...] = wc_ref[...].astype(BF16)
        wsb[...] = ws_ref[...].astype(BF16)
        wab[...] = wa_ref[...].astype(BF16)

    def term(u_ref, wb, g_ref):
        return _sigmoid(g_ref[...].astype(F32)) * jnp.dot(u_ref[...], wb[...], preferred_element_type=F32)

    o_ref[...] = (term(uc_ref, wcb, gc_ref) + term(us_ref, wsb, gs_ref)
                  + term(ua_ref, wab, ga_ref)).astype(o_ref.dtype)


def _merge(z, u_conv, u_sgu, u_att, w_conv_out, w_sgu_out, w_na_out, l, rows):
    tm, tn = 512, 512
    k = CONV_DIM
    act = pl.BlockSpec((tm, k), lambda j, i: (i, 0))
    wsp = pl.BlockSpec((None, k, tn), lambda j, i: (l, 0, j))
    gate = lambda g: pl.BlockSpec((tm, tn), lambda j, i, g=g: (i, (G_OFF + g * D_MODEL) // tn + j))
    return pl.pallas_call(
        _merge_kernel,
        grid=(D_MODEL // tn, rows // tm),
        in_specs=[act, act, act, wsp, wsp, wsp, gate(0), gate(1), gate(2)],
        out_specs=pl.BlockSpec((tm, tn), lambda j, i: (i, j)),
        out_shape=jax.ShapeDtypeStruct((rows, D_MODEL), BF16),
        scratch_shapes=[pltpu.VMEM((k, tn), BF16)] * 3,
        compiler_params=_params("arbitrary", "arbitrary"),
        name="merge",
    )(u_conv, u_sgu, u_att, w_conv_out, w_sgu_out, w_na_out, z, z, z)


def kernel(x, c, ctx, c_ctx, w_mod, b_mod, norm1_g, norm2_g, w_in, b_in, conv_w, conv_b, conv_ln_g, conv_ln_b, w_conv_out, sgu_ln_g, sgu_ln_b, sgu_w, sgu_b, w_sgu_out, na_rpb, w_na_out, w_o, b_o, w_ff1, b_ff1, w_ff2, b_ff2, final_g):
    xs = jnp.concatenate([x.reshape(T_LAT, D_MODEL), ctx.reshape(T_CTX, D_MODEL)], axis=0)
    c_rows = jnp.zeros((MOD_ROWS, D_MODEL), F32).at[:BATCH].set(c).at[BATCH].set(c_ctx)
    mods = _mod_table(c_rows, w_mod, b_mod)
    rpb_flat = na_rpb.reshape(-1)

    for l in range(DEPTH):
        rows = T_LAT if l == DEPTH - 1 else T_ALL
        h = _mod_norm(xs, norm1_g, mods, l, 0, T_ALL)
        z = _matmul(h, w_in, b_in, l, tm=T_ALL // 8, tn=1024, out_dtype=BF16, name="in_proj")
        u_conv = _conv_branch(z, conv_w, conv_b, conv_ln_g, conv_ln_b, l, rows)
        u_sgu = _sgu_branch(z, sgu_ln_g, sgu_ln_b, sgu_w, sgu_b, l, rows)
        u_att = _attention(z, rpb_flat, l, rows)
        merged = _merge(z, u_conv, u_sgu, u_att, w_conv_out, w_sgu_out, w_na_out, l, rows)
        xs = _matmul(merged, w_o, b_o, l, tm=512, tn=1024, out_dtype=F32, res=xs, mods=mods, k_gate=2,
                     name="out_proj")
        h2 = _mod_norm(xs, norm2_g, mods, l, 3, rows)
        a = _matmul(h2, w_ff1, b_ff1, l, tm=rows // 8, tn=1024, out_dtype=BF16, act="relu2", name="ffn1")
        xs = _matmul(a, w_ff2, b_ff2, l, tm=256, tn=512, out_dtype=F32, res=xs, mods=mods, k_gate=5,
                     name="ffn2")

    return _final_norm(xs, final_g).reshape(BATCH, SEQ, D_MODEL)
```

```python
import functools

import jax
import jax.numpy as jnp
import numpy as np
from jax import lax
from jax.experimental import pallas as pl
from jax.experimental.pallas import tpu as pltpu

F32 = jnp.float32
BF16 = jnp.bfloat16

D_MODEL = 2048
BATCH = 2
SEQ = 4096
DEPTH = 4
GRID_W = 64
GRID_ROWS = SEQ // GRID_W
CTX_LEN = 256
EPS = 1e-6
CONV_DIM = D_MODEL // 2
CONV_WIDTH = 31
CONV_PAD = CONV_WIDTH // 2
SGU_DIM = D_MODEL // 2
SGU_CHUNK = 128
SGU_GROUPS = SGU_DIM // 128
NA_HEAD_DIM = 64
NA_HEADS = (D_MODEL // 2) // NA_HEAD_DIM
NA_DIM = NA_HEADS * NA_HEAD_DIM
NA_KH = 8
NA_KW = 16
RPB_ROWS = 2 * NA_KH - 1
RPB_COLS = 2 * NA_KW - 1
N_BRANCH = 3
D_FF = 4 * D_MODEL
Q_OFF = 2 * CONV_DIM + 2 * SGU_DIM
K_OFF = Q_OFF + NA_DIM
V_OFF = K_OFF + NA_DIM
G_OFF = V_OFF + NA_DIM
IN_DIM = G_OFF + N_BRANCH * D_MODEL
NEG_INF = -1e30

T_LAT = BATCH * SEQ
T_CTX = BATCH * CTX_LEN
T_ALL = T_LAT + T_CTX
CTX_BASE = T_LAT
MOD_ROWS = 8
LANES = 128
HALO = 16
CONV_W_ROWS = 32
VMEM_LIMIT = 56 * 1024 * 1024


def _segment(row0):
    return jnp.minimum(row0 // SEQ, BATCH)


def _sigmoid(x):
    return 1.0 / (1.0 + jnp.exp(-x))


def _params(*sem):
    return pltpu.CompilerParams(dimension_semantics=sem, vmem_limit_bytes=VMEM_LIMIT)


def _layer_vec(stack, l, n):
    nd = lambda *_: (l, 0, 0)
    return pl.BlockSpec((None, 1, n), nd), stack.reshape(DEPTH, 1, n)


def _mod_kernel(c_ref, w_ref, b_ref, o_ref):
    c = c_ref[...]
    s = c * _sigmoid(c)
    o_ref[...] = jnp.dot(s.astype(BF16), w_ref[...].astype(BF16),
                         preferred_element_type=F32) + b_ref[...]


def _mod_table(c_rows, w_mod, b_mod):
    tn = 1024
    n = 6 * D_MODEL
    return pl.pallas_call(
        _mod_kernel,
        grid=(DEPTH, n // tn),
        in_specs=[pl.BlockSpec((MOD_ROWS, D_MODEL), lambda l, j: (0, 0)),
                  pl.BlockSpec((None, D_MODEL, tn), lambda l, j: (l, 0, j)),
                  pl.BlockSpec((None, 1, tn), lambda l, j: (l, 0, j))],
        out_specs=pl.BlockSpec((None, MOD_ROWS, tn), lambda l, j: (l, 0, j)),
        out_shape=jax.ShapeDtypeStruct((DEPTH, MOD_ROWS, n), F32),
        compiler_params=_params("arbitrary", "arbitrary"),
        name="mod_table",
    )(c_rows, w_mod, b_mod.reshape(DEPTH, 1, n))


def _mod_spec(l, k, tn=D_MODEL, col=None):
    per = D_MODEL // tn
    if col is None:
        return pl.BlockSpec((None, MOD_ROWS, tn), lambda *_: (l, 0, k * per))
    return pl.BlockSpec((None, MOD_ROWS, tn), lambda *ids: (l, 0, k * per + col(*ids)))


def _rms(x, g):
    return x * lax.rsqrt(jnp.mean(x * x, axis=-1, keepdims=True) + EPS) * g


def _norm_kernel(x_ref, g_ref, sh_ref, sc_ref, o_ref, *, tm):
    seg = _segment(pl.program_id(0) * tm)
    y = _rms(x_ref[...], g_ref[...])
    o_ref[...] = (y * (1.0 + sc_ref[pl.ds(seg, 1), :]) + sh_ref[pl.ds(seg, 1), :]).astype(o_ref.dtype)


def _mod_norm(x, g_stack, mods, l, k_shift, rows):
    tm = 512
    g_spec, g_arr = _layer_vec(g_stack, l, D_MODEL)
    return pl.pallas_call(
        functools.partial(_norm_kernel, tm=tm),
        grid=(rows // tm,),
        in_specs=[pl.BlockSpec((tm, D_MODEL), lambda i: (i, 0)), g_spec,
                  _mod_spec(l, k_shift), _mod_spec(l, k_shift + 1)],
        out_specs=pl.BlockSpec((tm, D_MODEL), lambda i: (i, 0)),
        out_shape=jax.ShapeDtypeStruct((rows, D_MODEL), BF16),
        compiler_params=_params("arbitrary"),
        name="mod_norm",
    )(x, g_arr, mods, mods)


def _final_norm_kernel(x_ref, g_ref, o_ref):
    o_ref[...] = _rms(x_ref[...], g_ref[...])


def _final_norm(x, g):
    tm = 512
    return pl.pallas_call(
        _final_norm_kernel,
        grid=(T_LAT // tm,),
        in_specs=[pl.BlockSpec((tm, D_MODEL), lambda i: (i, 0)),
                  pl.BlockSpec((1, D_MODEL), lambda i: (0, 0))],
        out_specs=pl.BlockSpec((tm, D_MODEL), lambda i: (i, 0)),
        out_shape=jax.ShapeDtypeStruct((T_LAT, D_MODEL), F32),
        compiler_params=_params("arbitrary"),
        name="final_norm",
    )(x, g.reshape(1, D_MODEL))


CAST_SLICES = 32


def _mm_kernel(*refs, tm, n_row_tiles, act, residual, cast_w, side):
    refs = list(refs)
    x_ref, w_ref, b_ref = refs[:3]
    del refs[:3]
    if residual:
        res_ref, gate_ref = refs[:2]
        del refs[:2]
    if side:
        side_in = refs.pop(0)
    o_ref = refs.pop(0)
    if side:
        side_out = refs.pop(0)
    j, i = pl.program_id(0), pl.program_id(1)

    if cast_w:
        wb_ref = refs.pop(0)

        @pl.when(i == 0)
        def _():
            wb_ref[...] = w_ref[...].astype(BF16)
    else:
        wb_ref = w_ref

    if side:
        @pl.when(j * n_row_tiles + i < CAST_SLICES)
        def _():
            side_out[...] = side_in[...].astype(BF16)

    acc = jnp.dot(x_ref[...], wb_ref[...], preferred_element_type=F32) + b_ref[...]
    if act == "relu2":
        acc = jnp.square(jnp.maximum(acc, 0.0))
    if residual:
        seg = _segment(i * tm)
        acc = res_ref[...] + gate_ref[pl.ds(seg, 1), :] * acc
    o_ref[...] = acc.astype(o_ref.dtype)


def _matmul(x, w, b_stack, l, *, tm, tn, out_dtype, act=None, res=None, mods=None, k_gate=None,
            side_w=None, name):
    m, k = x.shape
    cast_w = w.ndim == 3
    n = w.shape[-1]
    n_row_tiles = m // tm
    residual = res is not None
    side = side_w is not None
    if cast_w:
        w_spec = pl.BlockSpec((None, k, tn), lambda j, i: (l, 0, j))
    else:
        w_spec = pl.BlockSpec((k, tn), lambda j, i: (0, j), pipeline_mode=pl.Buffered(1))
    in_specs = [pl.BlockSpec((tm, k), lambda j, i: (i, 0)), w_spec,
                pl.BlockSpec((None, 1, tn), lambda j, i: (l, 0, j))]
    args = [x, w, b_stack.reshape(DEPTH, 1, n)]
    out_specs = [pl.BlockSpec((tm, tn), lambda j, i: (i, j))]
    out_shape = [jax.ShapeDtypeStruct((m, n), out_dtype)]
    if residual:
        in_specs += [pl.BlockSpec((tm, tn), lambda j, i: (i, j)),
                     _mod_spec(l, k_gate, tn, col=lambda j, i: j)]
        args += [res, mods]
    if side:
        sk, sn = side_w.shape[1:]
        rows = sk // CAST_SLICES
        step = lambda j, i: jnp.minimum(j * n_row_tiles + i, CAST_SLICES - 1)
        in_specs += [pl.BlockSpec((None, rows, sn), lambda j, i: (l, step(j, i), 0))]
        args += [side_w]
        out_specs += [pl.BlockSpec((rows, sn), lambda j, i: (step(j, i), 0))]
        out_shape += [jax.ShapeDtypeStruct((sk, sn), BF16)]
    outs = pl.pallas_call(
        functools.partial(_mm_kernel, tm=tm, n_row_tiles=n_row_tiles, act=act, residual=residual,
                          cast_w=cast_w, side=side),
        grid=(n // tn, n_row_tiles),
        in_specs=in_specs,
        out_specs=out_specs,
        out_shape=out_shape,
        scratch_shapes=[pltpu.VMEM((k, tn), BF16)] if cast_w else [],
        compiler_params=_params("arbitrary", "arbitrary"),
        name=name,
    )(*args)
    return outs if side else outs[0]


def _conv_kernel(zc_ref, zp_ref, zn_ref, w_ref, cb_ref, lg_ref, lb_ref, o_ref, ubuf, cbuf, *, tm, rc):
    row0 = pl.program_id(0) * tm
    end = row0 + tm
    seq_start = (row0 == 0) | (row0 == SEQ) | (row0 == T_LAT) | (row0 == T_LAT + CTX_LEN)
    seq_end = (end == SEQ) | (end == T_LAT) | (end == T_LAT + CTX_LEN) | (end == T_ALL)

    def glu(z):
        a = z.astype(F32)
        return a[:, :CONV_DIM] * _sigmoid(a[:, CONV_DIM:])

    ubuf[0:HALO, :] = glu(zp_ref[...]) * jnp.where(seq_start, 0.0, 1.0)
    ubuf[HALO:HALO + tm, :] = glu(zc_ref[...])
    ubuf[HALO + tm:HALO + tm + HALO, :] = glu(zn_ref[...]) * jnp.where(seq_end, 0.0, 1.0)

    tap0 = HALO - CONV_PAD
    sub = 8
    for lc in range(CONV_DIM // LANES):
        lanes = pl.ds(lc * LANES, LANES)
        for r0 in range(0, tm, rc):
            out = None
            for rem in range(sub):
                acc = None
                for s in range(rem, tap0 + CONV_WIDTH, sub):
                    if s < tap0:
                        continue
                    term = w_ref[pl.ds(s - tap0, 1), lanes] * ubuf[pl.ds(r0 + s - rem, rc + sub), lanes]
                    acc = term if acc is None else acc + term
                shifted = acc[rem:rem + rc]
                out = shifted if out is None else out + shifted
            cbuf[pl.ds(r0, rc), lanes] = out

    y = cbuf[...] + cb_ref[...]
    yc = y - jnp.mean(y, axis=-1, keepdims=True)
    var = jnp.mean(yc * yc, axis=-1, keepdims=True)
    y = yc * lax.rsqrt(var + EPS) * lg_ref[...] + lb_ref[...]
    o_ref[...] = (y * _sigmoid(y)).astype(o_ref.dtype)


def _conv_branch(z, conv_w, conv_b, ln_g, ln_b, l, rows):
    tm, rc = 256, 64
    hb = tm // HALO
    last = T_ALL // HALO - 1
    wpad = jnp.pad(conv_w, ((0, 0), (0, CONV_W_ROWS - CONV_WIDTH), (0, 0)))
    cb_spec, cb = _layer_vec(conv_b, l, CONV_DIM)
    lg_spec, lg = _layer_vec(ln_g, l, CONV_DIM)
    lb_spec, lb = _layer_vec(ln_b, l, CONV_DIM)
    return pl.pallas_call(
        functools.partial(_conv_kernel, tm=tm, rc=rc),
        grid=(rows // tm,),
        in_specs=[pl.BlockSpec((tm, 2 * CONV_DIM), lambda i: (i, 0)),
                  pl.BlockSpec((HALO, 2 * CONV_DIM), lambda i: (jnp.maximum(i * hb - 1, 0), 0)),
                  pl.BlockSpec((HALO, 2 * CONV_DIM), lambda i: (jnp.minimum((i + 1) * hb, last), 0)),
                  pl.BlockSpec((None, CONV_W_ROWS, CONV_DIM), lambda i: (l, 0, 0)),
                  cb_spec, lg_spec, lb_spec],
        out_specs=pl.BlockSpec((tm, CONV_DIM), lambda i: (i, 0)),
        out_shape=jax.ShapeDtypeStruct((rows, CONV_DIM), BF16),
        scratch_shapes=[pltpu.VMEM((tm + 2 * HALO, CONV_DIM), F32),
                        pltpu.VMEM((tm, CONV_DIM), F32)],
        compiler_params=_params("arbitrary"),
        name="conv_branch",
    )(z, z, z, wpad, cb, lg, lb)


def _gelu_tanh(x):
    return 0.5 * x * (1.0 + jnp.tanh(np.sqrt(2.0 / np.pi).astype(np.float32) * (x + 0.044715 * (x * x * x))))


def _sgu_kernel(z_ref, lg_ref, lb_ref, ws_ref, bs_ref, o_ref, vbuf, *, tm):
    v = _gelu_tanh(z_ref[:, SGU_DIM:].astype(F32))
    vc = v - jnp.mean(v, axis=-1, keepdims=True)
    var = jnp.mean(vc * vc, axis=-1, keepdims=True)
    vbuf[...] = (vc * lax.rsqrt(var + EPS) * lg_ref[...] + lb_ref[...]).astype(BF16)
    for c in range(tm // SGU_CHUNK):
        rows = pl.ds(c * SGU_CHUNK, SGU_CHUNK)
        for gi in range(SGU_GROUPS):
            lanes = pl.ds(gi * LANES, LANES)
            sv = jnp.dot(ws_ref[gi].astype(BF16), vbuf[rows, lanes], preferred_element_type=F32)
            u = _gelu_tanh(z_ref[rows, lanes].astype(F32))
            o_ref[rows, lanes] = (u * (sv + bs_ref[:, lanes])).astype(o_ref.dtype)


def _sgu_branch(z, ln_g, ln_b, w_s, b_s, l, rows):
    tm = 256
    bs_full = jnp.repeat(jnp.swapaxes(b_s, 1, 2), SGU_DIM // SGU_GROUPS, axis=2)
    lg_spec, lg = _layer_vec(ln_g, l, SGU_DIM)
    lb_spec, lb = _layer_vec(ln_b, l, SGU_DIM)
    return pl.pallas_call(
        functools.partial(_sgu_kernel, tm=tm),
        grid=(rows // tm,),
        in_specs=[pl.BlockSpec((tm, 2 * SGU_DIM), lambda i: (i, 1)), lg_spec, lb_spec,
                  pl.BlockSpec((None, SGU_GROUPS, SGU_CHUNK, SGU_CHUNK), lambda i: (l, 0, 0, 0)),
                  pl.BlockSpec((None, SGU_CHUNK, SGU_DIM), lambda i: (l, 0, 0))],
        out_specs=pl.BlockSpec((tm, SGU_DIM), lambda i: (i, 0)),
        out_shape=jax.ShapeDtypeStruct((rows, SGU_DIM), BF16),
        scratch_shapes=[pltpu.VMEM((tm, SGU_DIM), BF16)],
        compiler_params=_params("arbitrary"),
        name="sgu_branch",
    )(z, lg, lb, w_s, bs_full)


BIAS_PAIRS = 2 * NA_KH - 2


def _nt_dot(a, b):
    return lax.dot_general(a, b, (((1,), (1,)), ((), ())), preferred_element_type=F32)


def _attn_kernel(rpb_ref, q_ref, k_ref, v_ref, wc_ref, ws_ref, wa_ref, wo_ref,
                 o_ref, wcb_ref, wsb_ref, wab_ref, wob_ref, bias_s, vaug_s, *, layer, ctx_queries):
    pair = pl.program_id(0)
    wcb_ref[...] = wc_ref[...].astype(BF16)
    wsb_ref[...] = ws_ref[...].astype(BF16)
    wab_ref[...] = wa_ref[...].astype(BF16)
    wob_ref[...] = wo_ref[...].astype(BF16)

    lane_row = lax.broadcasted_iota(jnp.int32, (1, LANES), 1)
    lo = lane_row < NA_HEAD_DIM
    scale = NA_HEAD_DIM ** -0.5
    win_rows = NA_KH * GRID_W

    qi = lax.broadcasted_iota(jnp.int32, (GRID_W, LANES), 0)
    kcol = lax.broadcasted_iota(jnp.int32, (GRID_W, LANES), 1) & (GRID_W - 1)
    diff = kcol - qi + (NA_KW - 1)
    win = jnp.clip(qi - NA_KW // 2, 0, GRID_W - NA_KW)
    valid = (kcol >= win) & (kcol < win + NA_KW)

    def build(m, carry):
        for h in range(2):
            base = ((layer * NA_HEADS + 2 * pair + h) * RPB_ROWS + m) * RPB_COLS
            val = jnp.zeros((GRID_W, LANES), F32)
            for d in range(RPB_COLS):
                both = jnp.where(lo, rpb_ref[base + d], rpb_ref[base + RPB_COLS + d])
                val = jnp.where(diff == d, both, val)
            bias_s[m, h * GRID_W:(h + 1) * GRID_W, :] = jnp.where(valid, val, NEG_INF)
        return carry

    lax.fori_loop(0, BIAS_PAIRS, build, 0)

    vaug_s[:, :LANES] = v_ref[...]
    vaug_s[:, LANES:] = jnp.ones((T_ALL, LANES), BF16)

    def stack_heads(q):
        q = q * jnp.asarray(scale, q.dtype)
        zero = jnp.zeros_like(q)
        return jnp.concatenate([jnp.where(lo, q, zero), jnp.where(lo, zero, q)], axis=0)

    def softmax_pv(s_parts, v_parts, nq):
        tiles = [s[:, t * LANES:(t + 1) * LANES] for s in s_parts for t in range(s.shape[1] // LANES)]
        m = jnp.max(functools.reduce(jnp.maximum, tiles), axis=-1, keepdims=True)
        o2 = functools.reduce(jnp.add, [
            jnp.dot(jnp.exp(s - m).astype(BF16), v, preferred_element_type=F32)
            for s, v in zip(s_parts, v_parts)])
        o = o2[:, :LANES] / o2[:, LANES:]
        return jnp.where(lo, o[:nq], o[nq:])

    def latent_rows(b, r):
        cbase = CTX_BASE + b * CTX_LEN
        start = jnp.clip(r - NA_KH // 2, 0, GRID_ROWS - NA_KH)
        d0 = start - r + NA_KH - 1
        qrow = pl.multiple_of(b * SEQ + r * GRID_W, GRID_W)
        krow = pl.multiple_of(b * SEQ + start * GRID_W, GRID_W)
        q2 = stack_heads(q_ref[pl.ds(qrow, GRID_W), :])
        bias = jnp.concatenate([bias_s[d0 + 2 * t] for t in range(NA_KH // 2)], axis=1)
        s_loc = _nt_dot(q2, k_ref[pl.ds(krow, win_rows), :]) + bias
        s_ctx = _nt_dot(q2, k_ref[cbase:cbase + CTX_LEN, :])
        o = softmax_pv([s_loc, s_ctx],
                       [vaug_s[pl.ds(krow, win_rows), :], vaug_s[cbase:cbase + CTX_LEN, :]], GRID_W)
        o_ref[pl.ds(qrow, GRID_W), :] = o.astype(o_ref.dtype)

    def row_body(r, carry):
        for b in range(BATCH):
            latent_rows(b, r)
        return carry

    lax.fori_loop(0, GRID_ROWS, row_body, 0, unroll=8)

    for b in range(BATCH if ctx_queries else 0):
        cbase = CTX_BASE + b * CTX_LEN
        rows = slice(cbase, cbase + CTX_LEN)
        q2 = stack_heads(q_ref[rows, :])
        o = softmax_pv([_nt_dot(q2, k_ref[rows, :])], [vaug_s[rows, :]], CTX_LEN)
        o_ref[rows, :] = o.astype(o_ref.dtype)


def _attention(z, rpb_flat, w_conv_out, w_sgu_out, w_na_out, w_o, l, rows):
    steps = NA_HEADS // 2
    blk = lambda off: pl.BlockSpec((T_ALL, LANES), lambda p, off=off: (0, off // LANES + p))
    w_in = lambda k: pl.BlockSpec((None, k // steps, D_MODEL), lambda p: (l, p, 0))
    w_out = lambda k: pl.BlockSpec((k // steps, D_MODEL), lambda p: (p, 0))
    w_shape = lambda k: jax.ShapeDtypeStruct((k, D_MODEL), BF16)
    return pl.pallas_call(
        functools.partial(_attn_kernel, layer=l, ctx_queries=rows > T_LAT),
        grid=(steps,),
        in_specs=[pl.BlockSpec(memory_space=pltpu.SMEM), blk(Q_OFF), blk(K_OFF), blk(V_OFF),
                  w_in(CONV_DIM), w_in(SGU_DIM), w_in(NA_DIM), w_in(D_MODEL)],
        out_specs=[pl.BlockSpec((rows, LANES), lambda p: (0, p)),
                   w_out(CONV_DIM), w_out(SGU_DIM), w_out(NA_DIM), w_out(D_MODEL)],
        out_shape=[jax.ShapeDtypeStruct((rows, NA_DIM), BF16),
                   w_shape(CONV_DIM), w_shape(SGU_DIM), w_shape(NA_DIM), w_shape(D_MODEL)],
        scratch_shapes=[pltpu.VMEM((BIAS_PAIRS, 2 * GRID_W, LANES), F32),
                        pltpu.VMEM((T_ALL, 2 * LANES), BF16)],
        compiler_params=_params("arbitrary"),
        name="attention",
    )(rpb_flat, z, z, z, w_conv_out, w_sgu_out, w_na_out, w_o)


GATE_BLK = 1024


def _mix_kernel(*refs, tm):
    halves = D_MODEL // GATE_BLK
    u_refs = refs[:N_BRANCH]
    gate_refs = refs[N_BRANCH:N_BRANCH * (1 + halves)]
    w_refs = refs[N_BRANCH * (1 + halves):N_BRANCH * (2 + halves)]
    wo_ref, bo_ref, res_ref, gate1_ref, g_ref, sh_ref, sc_ref, x_out, h_out = refs[N_BRANCH * (2 + halves):]
    seg = _segment(pl.program_id(0) * tm)

    y = bo_ref[...]
    for hf in range(halves):
        cols = slice(hf * GATE_BLK, (hf + 1) * GATE_BLK)
        merged = None
        for b in range(N_BRANCH):
            gate = _sigmoid(gate_refs[b * halves + hf][...].astype(F32))
            term = gate * jnp.dot(u_refs[b][...], w_refs[b][:, cols], preferred_element_type=F32)
            merged = term if merged is None else merged + term
        y = y + jnp.dot(merged.astype(BF16), wo_ref[cols, :], preferred_element_type=F32)
    x = res_ref[...] + gate1_ref[pl.ds(seg, 1), :] * y
    x_out[...] = x
    h = _rms(x, g_ref[...])
    h_out[...] = (h * (1.0 + sc_ref[pl.ds(seg, 1), :]) + sh_ref[pl.ds(seg, 1), :]).astype(BF16)


def _mix(z, u_branches, w_branches, wob, b_o, xs, norm2_g, mods, l, rows):
    tm = 256
    halves = D_MODEL // GATE_BLK
    act = pl.BlockSpec((tm, CONV_DIM), lambda i: (i, 0))
    gate = lambda col: pl.BlockSpec((tm, GATE_BLK), lambda i: (i, col))
    gates = [gate(G_OFF // GATE_BLK + b * halves + hf) for b in range(N_BRANCH) for hf in range(halves)]
    resident = lambda k: pl.BlockSpec((k, D_MODEL), lambda i: (0, 0), pipeline_mode=pl.Buffered(1))
    row_blk = pl.BlockSpec((tm, D_MODEL), lambda i: (i, 0))
    bo_spec, bo = _layer_vec(b_o, l, D_MODEL)
    g_spec, g_arr = _layer_vec(norm2_g, l, D_MODEL)
    return pl.pallas_call(
        functools.partial(_mix_kernel, tm=tm),
        grid=(rows // tm,),
        in_specs=[act] * N_BRANCH + gates + [resident(CONV_DIM)] * N_BRANCH
        + [resident(D_MODEL), bo_spec, row_blk, _mod_spec(l, 2), g_spec, _mod_spec(l, 3), _mod_spec(l, 4)],
        out_specs=[row_blk, row_blk],
        out_shape=[jax.ShapeDtypeStruct((rows, D_MODEL), F32), jax.ShapeDtypeStruct((rows, D_MODEL), BF16)],
        compiler_params=_params("arbitrary"),
        name="mix",
    )(*u_branches, *([z] * (N_BRANCH * halves)), *w_branches, wob, bo, xs, mods, g_arr, mods, mods)


def kernel(x, c, ctx, c_ctx, w_mod, b_mod, norm1_g, norm2_g, w_in, b_in, conv_w, conv_b, conv_ln_g, conv_ln_b, w_conv_out, sgu_ln_g, sgu_ln_b, sgu_w, sgu_b, w_sgu_out, na_rpb, w_na_out, w_o, b_o, w_ff1, b_ff1, w_ff2, b_ff2, final_g):
    xs = jnp.concatenate([x.reshape(T_LAT, D_MODEL), ctx.reshape(T_CTX, D_MODEL)], axis=0)
    c_rows = jnp.zeros((MOD_ROWS, D_MODEL), F32).at[:BATCH].set(c).at[BATCH].set(c_ctx)
    mods = _mod_table(c_rows, w_mod, b_mod)
    rpb_flat = na_rpb.reshape(-1)

    for l in range(DEPTH):
        rows = T_LAT if l == DEPTH - 1 else T_ALL
        h = _mod_norm(xs, norm1_g, mods, l, 0, T_ALL)
        z = _matmul(h, w_in, b_in, l, tm=T_ALL // 8, tn=1024, out_dtype=BF16, name="in_proj")
        u_conv = _conv_branch(z, conv_w, conv_b, conv_ln_g, conv_ln_b, l, rows)
        u_sgu = _sgu_branch(z, sgu_ln_g, sgu_ln_b, sgu_w, sgu_b, l, rows)
        u_att, wcb, wsb, wab, wob = _attention(z, rpb_flat, w_conv_out, w_sgu_out, w_na_out, w_o, l, rows)
        xs, h2 = _mix(z, (u_conv, u_sgu, u_att), (wcb, wsb, wab), wob, b_o, xs, norm2_g, mods, l, rows)
        a, w2b = _matmul(h2, w_ff1, b_ff1, l, tm=rows // 8, tn=1024, out_dtype=BF16, act="relu2",
                         side_w=w_ff2, name="ffn1")
        xs = _matmul(a, w2b, b_ff2, l, tm=512, tn=1024, out_dtype=F32, res=xs, mods=mods, k_gate=5,
                     name="ffn2")

    return _final_norm(xs, final_g).reshape(BATCH, SEQ, D_MODEL)
```

```python
import functools

import jax
import jax.numpy as jnp
import numpy as np
from jax import lax
from jax.experimental import pallas as pl
from jax.experimental.pallas import tpu as pltpu

F32 = jnp.float32
BF16 = jnp.bfloat16

D_MODEL = 2048
BATCH = 2
SEQ = 4096
DEPTH = 4
GRID_W = 64
GRID_ROWS = SEQ // GRID_W
CTX_LEN = 256
EPS = 1e-6
CONV_DIM = D_MODEL // 2
CONV_WIDTH = 31
CONV_PAD = CONV_WIDTH // 2
SGU_DIM = D_MODEL // 2
SGU_CHUNK = 128
SGU_GROUPS = SGU_DIM // 128
NA_HEAD_DIM = 64
NA_HEADS = (D_MODEL // 2) // NA_HEAD_DIM
NA_DIM = NA_HEADS * NA_HEAD_DIM
NA_KH = 8
NA_KW = 16
RPB_ROWS = 2 * NA_KH - 1
RPB_COLS = 2 * NA_KW - 1
N_BRANCH = 3
D_FF = 4 * D_MODEL
Q_OFF = 2 * CONV_DIM + 2 * SGU_DIM
K_OFF = Q_OFF + NA_DIM
V_OFF = K_OFF + NA_DIM
G_OFF = V_OFF + NA_DIM
IN_DIM = G_OFF + N_BRANCH * D_MODEL
NEG_INF = -1e30

T_LAT = BATCH * SEQ
T_CTX = BATCH * CTX_LEN
T_ALL = T_LAT + T_CTX
CTX_BASE = T_LAT
MOD_ROWS = 8
LANES = 128
HALO = 16
CONV_W_ROWS = 32
VMEM_LIMIT = 56 * 1024 * 1024


def _segment(row0):
    return jnp.minimum(row0 // SEQ, BATCH)


def _sigmoid(x):
    return 1.0 / (1.0 + jnp.exp(-x))


def _params(*sem):
    return pltpu.CompilerParams(dimension_semantics=sem, vmem_limit_bytes=VMEM_LIMIT)


def _layer_vec(stack, l, n):
    nd = lambda *_: (l, 0, 0)
    return pl.BlockSpec((None, 1, n), nd), stack.reshape(DEPTH, 1, n)


def _mod_kernel(c_ref, w_ref, b_ref, o_ref):
    c = c_ref[...]
    s = c * _sigmoid(c)
    o_ref[...] = jnp.dot(s.astype(BF16), w_ref[...].astype(BF16),
                         preferred_element_type=F32) + b_ref[...]


def _mod_table(c_rows, w_mod, b_mod):
    tn = 1024
    n = 6 * D_MODEL
    return pl.pallas_call(
        _mod_kernel,
        grid=(DEPTH, n // tn),
        in_specs=[pl.BlockSpec((MOD_ROWS, D_MODEL), lambda l, j: (0, 0)),
                  pl.BlockSpec((None, D_MODEL, tn), lambda l, j: (l, 0, j)),
                  pl.BlockSpec((None, 1, tn), lambda l, j: (l, 0, j))],
        out_specs=pl.BlockSpec((None, MOD_ROWS, tn), lambda l, j: (l, 0, j)),
        out_shape=jax.ShapeDtypeStruct((DEPTH, MOD_ROWS, n), F32),
        compiler_params=_params("arbitrary", "arbitrary"),
        name="mod_table",
    )(c_rows, w_mod, b_mod.reshape(DEPTH, 1, n))


def _mod_spec(l, k, tn=D_MODEL, col=None):
    per = D_MODEL // tn
    if col is None:
        return pl.BlockSpec((None, MOD_ROWS, tn), lambda *_: (l, 0, k * per))
    return pl.BlockSpec((None, MOD_ROWS, tn), lambda *ids: (l, 0, k * per + col(*ids)))


def _rms(x, g):
    return x * lax.rsqrt(jnp.mean(x * x, axis=-1, keepdims=True) + EPS) * g


def _stream_tile(lat_ref, ctx_ref, row0):
    return jnp.where(row0 < T_LAT, lat_ref[...], ctx_ref[...])


def _split_stream_specs(tm):
    n_lat = T_LAT // tm
    return [pl.BlockSpec((tm, D_MODEL), lambda i: (jnp.minimum(i, n_lat - 1), 0)),
            pl.BlockSpec((tm, D_MODEL), lambda i: (jnp.maximum(i - n_lat, 0), 0))]


def _norm_kernel(*refs, tm, split):
    *x_refs, g_ref, sh_ref, sc_ref, o_ref = refs
    row0 = pl.program_id(0) * tm
    seg = _segment(row0)
    x = _stream_tile(*x_refs, row0) if split else x_refs[0][...]
    y = _rms(x, g_ref[...])
    o_ref[...] = (y * (1.0 + sc_ref[pl.ds(seg, 1), :]) + sh_ref[pl.ds(seg, 1), :]).astype(o_ref.dtype)


def _mod_norm(x, g_stack, mods, l, k_shift, rows):
    tm = 512
    g_spec, g_arr = _layer_vec(g_stack, l, D_MODEL)
    split = isinstance(x, tuple)
    x_specs = _split_stream_specs(tm) if split else [pl.BlockSpec((tm, D_MODEL), lambda i: (i, 0))]
    return pl.pallas_call(
        functools.partial(_norm_kernel, tm=tm, split=split),
        grid=(rows // tm,),
        in_specs=x_specs + [g_spec, _mod_spec(l, k_shift), _mod_spec(l, k_shift + 1)],
        out_specs=pl.BlockSpec((tm, D_MODEL), lambda i: (i, 0)),
        out_shape=jax.ShapeDtypeStruct((rows, D_MODEL), BF16),
        compiler_params=_params("arbitrary"),
        name="mod_norm",
    )(*(x if split else [x]), g_arr, mods, mods)


def _final_norm_kernel(x_ref, g_ref, o_ref):
    o_ref[...] = _rms(x_ref[...], g_ref[...])


def _final_norm(x, g):
    tm = 512
    return pl.pallas_call(
        _final_norm_kernel,
        grid=(T_LAT // tm,),
        in_specs=[pl.BlockSpec((tm, D_MODEL), lambda i: (i, 0)),
                  pl.BlockSpec((1, D_MODEL), lambda i: (0, 0))],
        out_specs=pl.BlockSpec((tm, D_MODEL), lambda i: (i, 0)),
        out_shape=jax.ShapeDtypeStruct((T_LAT, D_MODEL), F32),
        compiler_params=_params("arbitrary"),
        name="final_norm",
    )(x, g.reshape(1, D_MODEL))


CAST_SLICES = 32


def _mm_kernel(*refs, tm, n_row_tiles, act, residual, cast_w, side):
    refs = list(refs)
    x_ref, w_ref, b_ref = refs[:3]
    del refs[:3]
    if residual:
        res_ref, gate_ref = refs[:2]
        del refs[:2]
    if side:
        side_in = refs.pop(0)
    o_ref = refs.pop(0)
    if side:
        side_out = refs.pop(0)
    j, i = pl.program_id(0), pl.program_id(1)

    if cast_w:
        wb_ref = refs.pop(0)

        @pl.when(i == 0)
        def _():
            wb_ref[...] = w_ref[...].astype(BF16)
    else:
        wb_ref = w_ref

    if side:
        @pl.when(j * n_row_tiles + i < CAST_SLICES)
        def _():
            side_out[...] = side_in[...].astype(BF16)

    acc = jnp.dot(x_ref[...], wb_ref[...], preferred_element_type=F32) + b_ref[...]
    if act == "relu2":
        acc = jnp.square(jnp.maximum(acc, 0.0))
    if residual:
        seg = _segment(i * tm)
        acc = res_ref[...] + gate_ref[pl.ds(seg, 1), :] * acc
    o_ref[...] = acc.astype(o_ref.dtype)


def _matmul(x, w, b_stack, l, *, tm, tn, out_dtype, act=None, res=None, mods=None, k_gate=None,
            side_w=None, name):
    m, k = x.shape
    cast_w = w.ndim == 3
    n = w.shape[-1]
    n_row_tiles = m // tm
    residual = res is not None
    side = side_w is not None
    if cast_w:
        w_spec = pl.BlockSpec((None, k, tn), lambda j, i: (l, 0, j))
    else:
        w_spec = pl.BlockSpec((k, tn), lambda j, i: (0, j), pipeline_mode=pl.Buffered(1))
    in_specs = [pl.BlockSpec((tm, k), lambda j, i: (i, 0)), w_spec,
                pl.BlockSpec((None, 1, tn), lambda j, i: (l, 0, j))]
    args = [x, w, b_stack.reshape(DEPTH, 1, n)]
    out_specs = [pl.BlockSpec((tm, tn), lambda j, i: (i, j))]
    out_shape = [jax.ShapeDtypeStruct((m, n), out_dtype)]
    if residual:
        in_specs += [pl.BlockSpec((tm, tn), lambda j, i: (i, j)),
                     _mod_spec(l, k_gate, tn, col=lambda j, i: j)]
        args += [res, mods]
    if side:
        sk, sn = side_w.shape[1:]
        rows = sk // CAST_SLICES
        step = lambda j, i: jnp.minimum(j * n_row_tiles + i, CAST_SLICES - 1)
        in_specs += [pl.BlockSpec((None, rows, sn), lambda j, i: (l, step(j, i), 0))]
        args += [side_w]
        out_specs += [pl.BlockSpec((rows, sn), lambda j, i: (step(j, i), 0))]
        out_shape += [jax.ShapeDtypeStruct((sk, sn), BF16)]
    outs = pl.pallas_call(
        functools.partial(_mm_kernel, tm=tm, n_row_tiles=n_row_tiles, act=act, residual=residual,
                          cast_w=cast_w, side=side),
        grid=(n // tn, n_row_tiles),
        in_specs=in_specs,
        out_specs=out_specs,
        out_shape=out_shape,
        scratch_shapes=[pltpu.VMEM((k, tn), BF16)] if cast_w else [],
        compiler_params=_params("arbitrary", "arbitrary"),
        name=name,
    )(*args)
    return outs if side else outs[0]


def _conv_kernel(zc_ref, zp_ref, zn_ref, w_ref, cb_ref, lg_ref, lb_ref, o_ref, ubuf, cbuf, *, tm, rc):
    row0 = pl.program_id(0) * tm
    end = row0 + tm
    seq_start = (row0 == 0) | (row0 == SEQ) | (row0 == T_LAT) | (row0 == T_LAT + CTX_LEN)
    seq_end = (end == SEQ) | (end == T_LAT) | (end == T_LAT + CTX_LEN) | (end == T_ALL)

    def glu(z):
        a = z.astype(F32)
        return a[:, :CONV_DIM] * _sigmoid(a[:, CONV_DIM:])

    ubuf[0:HALO, :] = glu(zp_ref[...]) * jnp.where(seq_start, 0.0, 1.0)
    ubuf[HALO:HALO + tm, :] = glu(zc_ref[...])
    ubuf[HALO + tm:HALO + tm + HALO, :] = glu(zn_ref[...]) * jnp.where(seq_end, 0.0, 1.0)

    tap0 = HALO - CONV_PAD
    sub = 8
    for lc in range(CONV_DIM // LANES):
        lanes = pl.ds(lc * LANES, LANES)
        for r0 in range(0, tm, rc):
            out = None
            for rem in range(sub):
                acc = None
                for s in range(rem, tap0 + CONV_WIDTH, sub):
                    if s < tap0:
                        continue
                    term = w_ref[pl.ds(s - tap0, 1), lanes] * ubuf[pl.ds(r0 + s - rem, rc + sub), lanes]
                    acc = term if acc is None else acc + term
                shifted = acc[rem:rem + rc]
                out = shifted if out is None else out + shifted
            cbuf[pl.ds(r0, rc), lanes] = out

    y = cbuf[...] + cb_ref[...]
    yc = y - jnp.mean(y, axis=-1, keepdims=True)
    var = jnp.mean(yc * yc, axis=-1, keepdims=True)
    y = yc * lax.rsqrt(var + EPS) * lg_ref[...] + lb_ref[...]
    o_ref[...] = (y * _sigmoid(y)).astype(o_ref.dtype)


def _conv_branch(z, conv_w, conv_b, ln_g, ln_b, l, rows):
    tm, rc = CTX_LEN, 128
    hb = tm // HALO
    last = T_ALL // HALO - 1
    wpad = jnp.pad(conv_w, ((0, 0), (0, CONV_W_ROWS - CONV_WIDTH), (0, 0)))
    cb_spec, cb = _layer_vec(conv_b, l, CONV_DIM)
    lg_spec, lg = _layer_vec(ln_g, l, CONV_DIM)
    lb_spec, lb = _layer_vec(ln_b, l, CONV_DIM)
    return pl.pallas_call(
        functools.partial(_conv_kernel, tm=tm, rc=rc),
        grid=(rows // tm,),
        in_specs=[pl.BlockSpec((tm, 2 * CONV_DIM), lambda i: (i, 0)),
                  pl.BlockSpec((HALO, 2 * CONV_DIM), lambda i: (jnp.maximum(i * hb - 1, 0), 0)),
                  pl.BlockSpec((HALO, 2 * CONV_DIM), lambda i: (jnp.minimum((i + 1) * hb, last), 0)),
                  pl.BlockSpec((None, CONV_W_ROWS, CONV_DIM), lambda i: (l, 0, 0)),
                  cb_spec, lg_spec, lb_spec],
        out_specs=pl.BlockSpec((tm, CONV_DIM), lambda i: (i, 0)),
        out_shape=jax.ShapeDtypeStruct((rows, CONV_DIM), BF16),
        scratch_shapes=[pltpu.VMEM((tm + 2 * HALO, CONV_DIM), F32),
                        pltpu.VMEM((tm, CONV_DIM), F32)],
        compiler_params=_params("arbitrary"),
        name="conv_branch",
    )(z, z, z, wpad, cb, lg, lb)


def _gelu_tanh(x):
    return 0.5 * x * (1.0 + jnp.tanh(np.sqrt(2.0 / np.pi).astype(np.float32) * (x + 0.044715 * (x * x * x))))


def _sgu_kernel(z_ref, lg_ref, lb_ref, ws_ref, bs_ref, o_ref, vbuf, *, tm):
    v = _gelu_tanh(z_ref[:, SGU_DIM:].astype(F32))
    vc = v - jnp.mean(v, axis=-1, keepdims=True)
    var = jnp.mean(vc * vc, axis=-1, keepdims=True)
    vbuf[...] = (vc * lax.rsqrt(var + EPS) * lg_ref[...] + lb_ref[...]).astype(BF16)
    for c in range(tm // SGU_CHUNK):
        rows = pl.ds(c * SGU_CHUNK, SGU_CHUNK)
        for gi in range(SGU_GROUPS):
            lanes = pl.ds(gi * LANES, LANES)
            sv = jnp.dot(ws_ref[gi].astype(BF16), vbuf[rows, lanes], preferred_element_type=F32)
            u = _gelu_tanh(z_ref[rows, lanes].astype(F32))
            o_ref[rows, lanes] = (u * (sv + bs_ref[:, lanes])).astype(o_ref.dtype)


def _sgu_branch(z, ln_g, ln_b, w_s, b_s, l, rows):
    tm = 512
    bs_full = jnp.repeat(jnp.swapaxes(b_s, 1, 2), SGU_DIM // SGU_GROUPS, axis=2)
    lg_spec, lg = _layer_vec(ln_g, l, SGU_DIM)
    lb_spec, lb = _layer_vec(ln_b, l, SGU_DIM)
    return pl.pallas_call(
        functools.partial(_sgu_kernel, tm=tm),
        grid=(rows // tm,),
        in_specs=[pl.BlockSpec((tm, 2 * SGU_DIM), lambda i: (i, 1)), lg_spec, lb_spec,
                  pl.BlockSpec((None, SGU_GROUPS, SGU_CHUNK, SGU_CHUNK), lambda i: (l, 0, 0, 0)),
                  pl.BlockSpec((None, SGU_CHUNK, SGU_DIM), lambda i: (l, 0, 0))],
        out_specs=pl.BlockSpec((tm, SGU_DIM), lambda i: (i, 0)),
        out_shape=jax.ShapeDtypeStruct((rows, SGU_DIM), BF16),
        scratch_shapes=[pltpu.VMEM((tm, SGU_DIM), BF16)],
        compiler_params=_params("arbitrary"),
        name="sgu_branch",
    )(z, lg, lb, w_s, bs_full)


BIAS_PAIRS = 2 * NA_KH - 2


def _nt_dot(a, b):
    return lax.dot_general(a, b, (((1,), (1,)), ((), ())), preferred_element_type=F32)


def _attn_kernel(rpb_ref, q_ref, k_ref, v_ref, wc_ref, ws_ref, wa_ref, wo_ref,
                 o_ref, wcb_ref, wsb_ref, wab_ref, wob_ref, bias_s, vaug_s, *, layer, ctx_queries):
    pair = pl.program_id(0)
    wcb_ref[...] = wc_ref[...].astype(BF16)
    wsb_ref[...] = ws_ref[...].astype(BF16)
    wab_ref[...] = wa_ref[...].astype(BF16)
    wob_ref[...] = wo_ref[...].astype(BF16)

    lane_row = lax.broadcasted_iota(jnp.int32, (1, LANES), 1)
    lo = lane_row < NA_HEAD_DIM
    scale = NA_HEAD_DIM ** -0.5
    win_rows = NA_KH * GRID_W

    qi = lax.broadcasted_iota(jnp.int32, (GRID_W, LANES), 0)
    kcol = lax.broadcasted_iota(jnp.int32, (GRID_W, LANES), 1) & (GRID_W - 1)
    diff = kcol - qi + (NA_KW - 1)
    win = jnp.clip(qi - NA_KW // 2, 0, GRID_W - NA_KW)
    valid = (kcol >= win) & (kcol < win + NA_KW)

    def build(m, carry):
        for h in range(2):
            base = ((layer * NA_HEADS + 2 * pair + h) * RPB_ROWS + m) * RPB_COLS
            val = jnp.zeros((GRID_W, LANES), F32)
            for d in range(RPB_COLS):
                both = jnp.where(lo, rpb_ref[base + d], rpb_ref[base + RPB_COLS + d])
                val = jnp.where(diff == d, both, val)
            bias_s[m, h * GRID_W:(h + 1) * GRID_W, :] = jnp.where(valid, val, NEG_INF)
        return carry

    lax.fori_loop(0, BIAS_PAIRS, build, 0)

    vaug_s[:, :LANES] = v_ref[...]
    vaug_s[:, LANES:] = jnp.ones((T_ALL, LANES), BF16)

    def stack_heads(q):
        q = q * jnp.asarray(scale, q.dtype)
        zero = jnp.zeros_like(q)
        return jnp.concatenate([jnp.where(lo, q, zero), jnp.where(lo, zero, q)], axis=0)

    def softmax_pv(s_parts, v_parts, nq):
        tiles = [s[:, t * LANES:(t + 1) * LANES] for s in s_parts for t in range(s.shape[1] // LANES)]
        m = jnp.max(functools.reduce(jnp.maximum, tiles), axis=-1, keepdims=True)
        o2 = functools.reduce(jnp.add, [
            jnp.dot(jnp.exp(s - m).astype(BF16), v, preferred_element_type=F32)
            for s, v in zip(s_parts, v_parts)])
        o = o2[:, :LANES] / o2[:, LANES:]
        return jnp.where(lo, o[:nq], o[nq:])

    def latent_rows(b, r):
        cbase = CTX_BASE + b * CTX_LEN
        start = jnp.clip(r - NA_KH // 2, 0, GRID_ROWS - NA_KH)
        d0 = start - r + NA_KH - 1
        qrow = pl.multiple_of(b * SEQ + r * GRID_W, GRID_W)
        krow = pl.multiple_of(b * SEQ + start * GRID_W, GRID_W)
        q2 = stack_heads(q_ref[pl.ds(qrow, GRID_W), :])
        bias = jnp.concatenate([bias_s[d0 + 2 * t] for t in range(NA_KH // 2)], axis=1)
        s_loc = _nt_dot(q2, k_ref[pl.ds(krow, win_rows), :]) + bias
        s_ctx = _nt_dot(q2, k_ref[cbase:cbase + CTX_LEN, :])
        o = softmax_pv([s_loc, s_ctx],
                       [vaug_s[pl.ds(krow, win_rows), :], vaug_s[cbase:cbase + CTX_LEN, :]], GRID_W)
        o_ref[pl.ds(qrow, GRID_W), :] = o.astype(o_ref.dtype)

    def row_body(r, carry):
        for b in range(BATCH):
            latent_rows(b, r)
        return carry

    lax.fori_loop(0, GRID_ROWS, row_body, 0, unroll=16)

    for b in range(BATCH if ctx_queries else 0):
        cbase = CTX_BASE + b * CTX_LEN
        rows = slice(cbase, cbase + CTX_LEN)
        q2 = stack_heads(q_ref[rows, :])
        o = softmax_pv([_nt_dot(q2, k_ref[rows, :])], [vaug_s[rows, :]], CTX_LEN)
        o_ref[rows, :] = o.astype(o_ref.dtype)


def _attention(z, rpb_flat, w_conv_out, w_sgu_out, w_na_out, w_o, l, rows):
    steps = NA_HEADS // 2
    blk = lambda off: pl.BlockSpec((T_ALL, LANES), lambda p, off=off: (0, off // LANES + p))
    w_in = lambda k: pl.BlockSpec((None, k // steps, D_MODEL), lambda p: (l, p, 0))
    w_out = lambda k: pl.BlockSpec((k // steps, D_MODEL), lambda p: (p, 0))
    w_shape = lambda k: jax.ShapeDtypeStruct((k, D_MODEL), BF16)
    return pl.pallas_call(
        functools.partial(_attn_kernel, layer=l, ctx_queries=rows > T_LAT),
        grid=(steps,),
        in_specs=[pl.BlockSpec(memory_space=pltpu.SMEM), blk(Q_OFF), blk(K_OFF), blk(V_OFF),
                  w_in(CONV_DIM), w_in(SGU_DIM), w_in(NA_DIM), w_in(D_MODEL)],
        out_specs=[pl.BlockSpec((rows, LANES), lambda p: (0, p)),
                   w_out(CONV_DIM), w_out(SGU_DIM), w_out(NA_DIM), w_out(D_MODEL)],
        out_shape=[jax.ShapeDtypeStruct((rows, NA_DIM), BF16),
                   w_shape(CONV_DIM), w_shape(SGU_DIM), w_shape(NA_DIM), w_shape(D_MODEL)],
        scratch_shapes=[pltpu.VMEM((BIAS_PAIRS, 2 * GRID_W, LANES), F32),
                        pltpu.VMEM((T_ALL, 2 * LANES), BF16)],
        compiler_params=_params("arbitrary"),
        name="attention",
    )(rpb_flat, z, z, z, w_conv_out, w_sgu_out, w_na_out, w_o)


GATE_BLK = 1024


def _mix_kernel(*refs, tm, split):
    halves = D_MODEL // GATE_BLK
    u_refs = refs[:N_BRANCH]
    gate_refs = refs[N_BRANCH:N_BRANCH * (1 + halves)]
    w_refs = refs[N_BRANCH * (1 + halves):N_BRANCH * (2 + halves)]
    wo_ref, bo_ref, *res_refs, gate1_ref, g_ref, sh_ref, sc_ref, x_out, h_out = refs[N_BRANCH * (2 + halves):]
    row0 = pl.program_id(0) * tm
    seg = _segment(row0)

    y = bo_ref[...]
    for hf in range(halves):
        cols = slice(hf * GATE_BLK, (hf + 1) * GATE_BLK)
        merged = None
        for b in range(N_BRANCH):
            gate = _sigmoid(gate_refs[b * halves + hf][...].astype(F32))
            term = gate * jnp.dot(u_refs[b][...], w_refs[b][:, cols], preferred_element_type=F32)
            merged = term if merged is None else merged + term
        y = y + jnp.dot(merged.astype(BF16), wo_ref[cols, :], preferred_element_type=F32)
    res = _stream_tile(*res_refs, row0) if split else res_refs[0][...]
    x = res + gate1_ref[pl.ds(seg, 1), :] * y
    x_out[...] = x
    h = _rms(x, g_ref[...])
    h_out[...] = (h * (1.0 + sc_ref[pl.ds(seg, 1), :]) + sh_ref[pl.ds(seg, 1), :]).astype(BF16)


def _mix(z, u_branches, w_branches, wob, b_o, xs, norm2_g, mods, l, rows):
    tm = 256
    split = isinstance(xs, tuple)
    halves = D_MODEL // GATE_BLK
    act = pl.BlockSpec((tm, CONV_DIM), lambda i: (i, 0))
    gate = lambda col: pl.BlockSpec((tm, GATE_BLK), lambda i: (i, col))
    gates = [gate(G_OFF // GATE_BLK + b * halves + hf) for b in range(N_BRANCH) for hf in range(halves)]
    resident = lambda k: pl.BlockSpec((k, D_MODEL), lambda i: (0, 0), pipeline_mode=pl.Buffered(1))
    row_blk = pl.BlockSpec((tm, D_MODEL), lambda i: (i, 0))
    bo_spec, bo = _layer_vec(b_o, l, D_MODEL)
    g_spec, g_arr = _layer_vec(norm2_g, l, D_MODEL)
    return pl.pallas_call(
        functools.partial(_mix_kernel, tm=tm, split=split),
        grid=(rows // tm,),
        in_specs=[act] * N_BRANCH + gates + [resident(CONV_DIM)] * N_BRANCH + [resident(D_MODEL), bo_spec]
        + (_split_stream_specs(tm) if split else [row_blk])
        + [_mod_spec(l, 2), g_spec, _mod_spec(l, 3), _mod_spec(l, 4)],
        out_specs=[row_blk, row_blk],
        out_shape=[jax.ShapeDtypeStruct((rows, D_MODEL), F32), jax.ShapeDtypeStruct((rows, D_MODEL), BF16)],
        compiler_params=_params("arbitrary"),
        name="mix",
    )(*u_branches, *([z] * (N_BRANCH * halves)), *w_branches, wob, bo, *(xs if split else [xs]),
      mods, g_arr, mods, mods)


def kernel(x, c, ctx, c_ctx, w_mod, b_mod, norm1_g, norm2_g, w_in, b_in, conv_w, conv_b, conv_ln_g, conv_ln_b, w_conv_out, sgu_ln_g, sgu_ln_b, sgu_w, sgu_b, w_sgu_out, na_rpb, w_na_out, w_o, b_o, w_ff1, b_ff1, w_ff2, b_ff2, final_g):
    xs = (x.reshape(T_LAT, D_MODEL), ctx.reshape(T_CTX, D_MODEL))
    c_rows = jnp.zeros((MOD_ROWS, D_MODEL), F32).at[:BATCH].set(c).at[BATCH].set(c_ctx)
    mods = _mod_table(c_rows, w_mod, b_mod)
    rpb_flat = na_rpb.reshape(-1)

    for l in range(DEPTH):
        rows = T_LAT if l == DEPTH - 1 else T_ALL
        h = _mod_norm(xs, norm1_g, mods, l, 0, T_ALL)
        z = _matmul(h, w_in, b_in, l, tm=T_ALL // 8, tn=1024, out_dtype=BF16, name="in_proj")
        u_conv = _conv_branch(z, conv_w, conv_b, conv_ln_g, conv_ln_b, l, rows)
        u_sgu = _sgu_branch(z, sgu_ln_g, sgu_ln_b, sgu_w, sgu_b, l, rows)
        u_att, wcb, wsb, wab, wob = _attention(z, rpb_flat, w_conv_out, w_sgu_out, w_na_out, w_o, l, rows)
        xs, h2 = _mix(z, (u_conv, u_sgu, u_att), (wcb, wsb, wab), wob, b_o, xs, norm2_g, mods, l, rows)
        a, w2b = _matmul(h2, w_ff1, b_ff1, l, tm=rows // 8, tn=1024, out_dtype=BF16, act="relu2",
                         side_w=w_ff2, name="ffn1")
        xs = _matmul(a, w2b, b_ff2, l, tm=512, tn=1024, out_dtype=F32, res=xs, mods=mods, k_gate=5,
                     name="ffn2")

    return _final_norm(xs, final_g).reshape(BATCH, SEQ, D_MODEL)
```

```python
import functools

import jax
import jax.numpy as jnp
import numpy as np
from jax import lax
from jax.experimental import pallas as pl
from jax.experimental.pallas import tpu as pltpu

F32 = jnp.float32
BF16 = jnp.bfloat16

D_MODEL = 2048
BATCH = 2
SEQ = 4096
DEPTH = 4
GRID_W = 64
GRID_ROWS = SEQ // GRID_W
CTX_LEN = 256
EPS = 1e-6
CONV_DIM = D_MODEL // 2
CONV_WIDTH = 31
CONV_PAD = CONV_WIDTH // 2
SGU_DIM = D_MODEL // 2
SGU_CHUNK = 128
SGU_GROUPS = SGU_DIM // 128
NA_HEAD_DIM = 64
NA_HEADS = (D_MODEL // 2) // NA_HEAD_DIM
NA_DIM = NA_HEADS * NA_HEAD_DIM
NA_KH = 8
NA_KW = 16
RPB_ROWS = 2 * NA_KH - 1
RPB_COLS = 2 * NA_KW - 1
N_BRANCH = 3
D_FF = 4 * D_MODEL
Q_OFF = 2 * CONV_DIM + 2 * SGU_DIM
K_OFF = Q_OFF + NA_DIM
V_OFF = K_OFF + NA_DIM
G_OFF = V_OFF + NA_DIM
IN_DIM = G_OFF + N_BRANCH * D_MODEL
NEG_INF = -1e30

T_LAT = BATCH * SEQ
T_CTX = BATCH * CTX_LEN
T_ALL = T_LAT + T_CTX
CTX_BASE = T_LAT
MOD_ROWS = 8
LANES = 128
HALO = 16
CONV_W_ROWS = 32
VMEM_LIMIT = 60 * 1024 * 1024


def _segment(row0):
    return jnp.minimum(row0 // SEQ, BATCH)


def _sigmoid(x):
    return 1.0 / (1.0 + jnp.exp(-x))


def _params(*sem):
    return pltpu.CompilerParams(dimension_semantics=sem, vmem_limit_bytes=VMEM_LIMIT)


def _layer_vec(stack, l, n):
    nd = lambda *_: (l, 0, 0)
    return pl.BlockSpec((None, 1, n), nd), stack.reshape(DEPTH, 1, n)


def _mod_kernel(c_ref, w_ref, b_ref, o_ref):
    c = c_ref[...]
    s = c * _sigmoid(c)
    o_ref[...] = jnp.dot(s.astype(BF16), w_ref[...].astype(BF16),
                         preferred_element_type=F32) + b_ref[...]


def _mod_table(c_rows, w_mod, b_mod):
    tn = 1024
    n = 6 * D_MODEL
    return pl.pallas_call(
        _mod_kernel,
        grid=(DEPTH, n // tn),
        in_specs=[pl.BlockSpec((MOD_ROWS, D_MODEL), lambda l, j: (0, 0)),
                  pl.BlockSpec((None, D_MODEL, tn), lambda l, j: (l, 0, j)),
                  pl.BlockSpec((None, 1, tn), lambda l, j: (l, 0, j))],
        out_specs=pl.BlockSpec((None, MOD_ROWS, tn), lambda l, j: (l, 0, j)),
        out_shape=jax.ShapeDtypeStruct((DEPTH, MOD_ROWS, n), F32),
        compiler_params=_params("arbitrary", "arbitrary"),
        name="mod_table",
    )(c_rows, w_mod, b_mod.reshape(DEPTH, 1, n))


def _mod_spec(l, k, tn=D_MODEL, col=None):
    per = D_MODEL // tn
    if col is None:
        return pl.BlockSpec((None, MOD_ROWS, tn), lambda *_: (l, 0, k * per))
    return pl.BlockSpec((None, MOD_ROWS, tn), lambda *ids: (l, 0, k * per + col(*ids)))


def _rms(x, g):
    return x * lax.rsqrt(jnp.mean(x * x, axis=-1, keepdims=True) + EPS) * g


def _stream_tile(lat_ref, ctx_ref, row0):
    return jnp.where(row0 < T_LAT, lat_ref[...], ctx_ref[...])


def _split_stream_specs(tm):
    n_lat = T_LAT // tm
    return [pl.BlockSpec((tm, D_MODEL), lambda i: (jnp.minimum(i, n_lat - 1), 0)),
            pl.BlockSpec((tm, D_MODEL), lambda i: (jnp.maximum(i - n_lat, 0), 0))]


def _norm_kernel(*refs, tm, split):
    *x_refs, g_ref, sh_ref, sc_ref, o_ref = refs
    row0 = pl.program_id(0) * tm
    seg = _segment(row0)
    x = _stream_tile(*x_refs, row0) if split else x_refs[0][...]
    y = _rms(x, g_ref[...])
    o_ref[...] = (y * (1.0 + sc_ref[pl.ds(seg, 1), :]) + sh_ref[pl.ds(seg, 1), :]).astype(o_ref.dtype)


def _mod_norm(x, g_stack, mods, l, k_shift, rows):
    tm = 512
    g_spec, g_arr = _layer_vec(g_stack, l, D_MODEL)
    split = isinstance(x, tuple)
    x_specs = _split_stream_specs(tm) if split else [pl.BlockSpec((tm, D_MODEL), lambda i: (i, 0))]
    return pl.pallas_call(
        functools.partial(_norm_kernel, tm=tm, split=split),
        grid=(rows // tm,),
        in_specs=x_specs + [g_spec, _mod_spec(l, k_shift), _mod_spec(l, k_shift + 1)],
        out_specs=pl.BlockSpec((tm, D_MODEL), lambda i: (i, 0)),
        out_shape=jax.ShapeDtypeStruct((rows, D_MODEL), BF16),
        compiler_params=_params("arbitrary"),
        name="mod_norm",
    )(*(x if split else [x]), g_arr, mods, mods)


def _final_norm_kernel(x_ref, g_ref, o_ref):
    o_ref[...] = _rms(x_ref[...], g_ref[...])


def _final_norm(x, g):
    tm = 512
    return pl.pallas_call(
        _final_norm_kernel,
        grid=(T_LAT // tm,),
        in_specs=[pl.BlockSpec((tm, D_MODEL), lambda i: (i, 0)),
                  pl.BlockSpec((1, D_MODEL), lambda i: (0, 0))],
        out_specs=pl.BlockSpec((tm, D_MODEL), lambda i: (i, 0)),
        out_shape=jax.ShapeDtypeStruct((T_LAT, D_MODEL), F32),
        compiler_params=_params("arbitrary"),
        name="final_norm",
    )(x, g.reshape(1, D_MODEL))


def _mm_kernel(*refs, tm, act, residual, cast_w):
    refs = list(refs)
    x_ref, w_ref, b_ref = refs[:3]
    del refs[:3]
    if residual:
        res_ref, gate_ref = refs[:2]
        del refs[:2]
    o_ref = refs.pop(0)
    i = pl.program_id(1)

    if cast_w:
        wb_ref = refs.pop(0)

        @pl.when(i == 0)
        def _():
            wb_ref[...] = w_ref[...].astype(BF16)
    else:
        wb_ref = w_ref

    acc = jnp.dot(x_ref[...], wb_ref[...], preferred_element_type=F32) + b_ref[...]
    if act == "relu2":
        acc = jnp.square(jnp.maximum(acc, 0.0))
    if residual:
        seg = _segment(i * tm)
        acc = res_ref[...] + gate_ref[pl.ds(seg, 1), :] * acc
    o_ref[...] = acc.astype(o_ref.dtype)


def _matmul(x, w, b_stack, l, *, tm, tn, out_dtype, act=None, res=None, mods=None, k_gate=None, name):
    m, k = x.shape
    cast_w = w.ndim == 3
    n = w.shape[-1]
    residual = res is not None
    if cast_w:
        w_spec = pl.BlockSpec((None, k, tn), lambda j, i: (l, 0, j))
    else:
        w_spec = pl.BlockSpec((k, tn), lambda j, i: (0, j), pipeline_mode=pl.Buffered(1))
    in_specs = [pl.BlockSpec((tm, k), lambda j, i: (i, 0)), w_spec,
                pl.BlockSpec((None, 1, tn), lambda j, i: (l, 0, j))]
    args = [x, w, b_stack.reshape(DEPTH, 1, n)]
    if residual:
        in_specs += [pl.BlockSpec((tm, tn), lambda j, i: (i, j)),
                     _mod_spec(l, k_gate, tn, col=lambda j, i: j)]
        args += [res, mods]
    return pl.pallas_call(
        functools.partial(_mm_kernel, tm=tm, act=act, residual=residual, cast_w=cast_w),
        grid=(n // tn, m // tm),
        in_specs=in_specs,
        out_specs=pl.BlockSpec((tm, tn), lambda j, i: (i, j)),
        out_shape=jax.ShapeDtypeStruct((m, n), out_dtype),
        scratch_shapes=[pltpu.VMEM((k, tn), BF16)] if cast_w else [],
        compiler_params=_params("arbitrary", "arbitrary"),
        name=name,
    )(*args)


def _conv_kernel(zc_ref, zp_ref, zn_ref, w_ref, cb_ref, lg_ref, lb_ref, o_ref, ubuf, cbuf, *, tm, rc):
    row0 = pl.program_id(0) * tm
    end = row0 + tm
    seq_start = (row0 == 0) | (row0 == SEQ) | (row0 == T_LAT) | (row0 == T_LAT + CTX_LEN)
    seq_end = (end == SEQ) | (end == T_LAT) | (end == T_LAT + CTX_LEN) | (end == T_ALL)

    def glu(z):
        a = z.astype(F32)
        return a[:, :CONV_DIM] * _sigmoid(a[:, CONV_DIM:])

    ubuf[0:HALO, :] = glu(zp_ref[...]) * jnp.where(seq_start, 0.0, 1.0)
    ubuf[HALO:HALO + tm, :] = glu(zc_ref[...])
    ubuf[HALO + tm:HALO + tm + HALO, :] = glu(zn_ref[...]) * jnp.where(seq_end, 0.0, 1.0)

    tap0 = HALO - CONV_PAD
    sub = 8
    for lc in range(CONV_DIM // LANES):
        lanes = pl.ds(lc * LANES, LANES)
        for r0 in range(0, tm, rc):
            out = None
            for rem in range(sub):
                acc = None
                for s in range(rem, tap0 + CONV_WIDTH, sub):
                    if s < tap0:
                        continue
                    term = w_ref[pl.ds(s - tap0, 1), lanes] * ubuf[pl.ds(r0 + s - rem, rc + sub), lanes]
                    acc = term if acc is None else acc + term
                shifted = acc[rem:rem + rc]
                out = shifted if out is None else out + shifted
            cbuf[pl.ds(r0, rc), lanes] = out

    y = cbuf[...] + cb_ref[...]
    yc = y - jnp.mean(y, axis=-1, keepdims=True)
    var = jnp.mean(yc * yc, axis=-1, keepdims=True)
    y = yc * lax.rsqrt(var + EPS) * lg_ref[...] + lb_ref[...]
    o_ref[...] = (y * _sigmoid(y)).astype(o_ref.dtype)


def _conv_branch(z, conv_w, conv_b, ln_g, ln_b, l, rows):
    tm, rc = CTX_LEN, 128
    hb = tm // HALO
    last = T_ALL // HALO - 1
    wpad = jnp.pad(conv_w, ((0, 0), (0, CONV_W_ROWS - CONV_WIDTH), (0, 0)))
    cb_spec, cb = _layer_vec(conv_b, l, CONV_DIM)
    lg_spec, lg = _layer_vec(ln_g, l, CONV_DIM)
    lb_spec, lb = _layer_vec(ln_b, l, CONV_DIM)
    return pl.pallas_call(
        functools.partial(_conv_kernel, tm=tm, rc=rc),
        grid=(rows // tm,),
        in_specs=[pl.BlockSpec((tm, 2 * CONV_DIM), lambda i: (i, 0)),
                  pl.BlockSpec((HALO, 2 * CONV_DIM), lambda i: (jnp.maximum(i * hb - 1, 0), 0)),
                  pl.BlockSpec((HALO, 2 * CONV_DIM), lambda i: (jnp.minimum((i + 1) * hb, last), 0)),
                  pl.BlockSpec((None, CONV_W_ROWS, CONV_DIM), lambda i: (l, 0, 0)),
                  cb_spec, lg_spec, lb_spec],
        out_specs=pl.BlockSpec((tm, CONV_DIM), lambda i: (i, 0)),
        out_shape=jax.ShapeDtypeStruct((rows, CONV_DIM), BF16),
        scratch_shapes=[pltpu.VMEM((tm + 2 * HALO, CONV_DIM), F32),
                        pltpu.VMEM((tm, CONV_DIM), F32)],
        compiler_params=_params("arbitrary"),
        name="conv_branch",
    )(z, z, z, wpad, cb, lg, lb)


def _gelu_tanh(x):
    return 0.5 * x * (1.0 + jnp.tanh(np.sqrt(2.0 / np.pi).astype(np.float32) * (x + 0.044715 * (x * x * x))))


def _sgu_kernel(z_ref, lg_ref, lb_ref, ws_ref, bs_ref, o_ref, vbuf, *, tm):
    v = _gelu_tanh(z_ref[:, SGU_DIM:].astype(F32))
    vc = v - jnp.mean(v, axis=-1, keepdims=True)
    var = jnp.mean(vc * vc, axis=-1, keepdims=True)
    vbuf[...] = (vc * lax.rsqrt(var + EPS) * lg_ref[...] + lb_ref[...]).astype(BF16)
    for c in range(tm // SGU_CHUNK):
        rows = pl.ds(c * SGU_CHUNK, SGU_CHUNK)
        for gi in range(SGU_GROUPS):
            lanes = pl.ds(gi * LANES, LANES)
            sv = jnp.dot(ws_ref[gi].astype(BF16), vbuf[rows, lanes], preferred_element_type=F32)
            u = _gelu_tanh(z_ref[rows, lanes].astype(F32))
            o_ref[rows, lanes] = (u * (sv + bs_ref[:, lanes])).astype(o_ref.dtype)


def _sgu_branch(z, ln_g, ln_b, w_s, b_s, l, rows):
    tm = 512
    bs_full = jnp.repeat(jnp.swapaxes(b_s, 1, 2), SGU_DIM // SGU_GROUPS, axis=2)
    lg_spec, lg = _layer_vec(ln_g, l, SGU_DIM)
    lb_spec, lb = _layer_vec(ln_b, l, SGU_DIM)
    return pl.pallas_call(
        functools.partial(_sgu_kernel, tm=tm),
        grid=(rows // tm,),
        in_specs=[pl.BlockSpec((tm, 2 * SGU_DIM), lambda i: (i, 1)), lg_spec, lb_spec,
                  pl.BlockSpec((None, SGU_GROUPS, SGU_CHUNK, SGU_CHUNK), lambda i: (l, 0, 0, 0)),
                  pl.BlockSpec((None, SGU_CHUNK, SGU_DIM), lambda i: (l, 0, 0))],
        out_specs=pl.BlockSpec((tm, SGU_DIM), lambda i: (i, 0)),
        out_shape=jax.ShapeDtypeStruct((rows, SGU_DIM), BF16),
        scratch_shapes=[pltpu.VMEM((tm, SGU_DIM), BF16)],
        compiler_params=_params("arbitrary"),
        name="sgu_branch",
    )(z, lg, lb, w_s, bs_full)


BIAS_PAIRS = 2 * NA_KH - 2


def _nt_dot(a, b):
    return lax.dot_general(a, b, (((1,), (1,)), ((), ())), preferred_element_type=F32)


def _attn_kernel(rpb_ref, q_ref, k_ref, v_ref, wc_ref, ws_ref, wa_ref, wo_ref,
                 o_ref, wcb_ref, wsb_ref, wab_ref, wob_ref, bias_s, vaug_s, *, layer, ctx_queries):
    pair = pl.program_id(0)
    wcb_ref[...] = wc_ref[...].astype(BF16)
    wsb_ref[...] = ws_ref[...].astype(BF16)
    wab_ref[...] = wa_ref[...].astype(BF16)
    wob_ref[...] = wo_ref[...].astype(BF16)

    lane_row = lax.broadcasted_iota(jnp.int32, (1, LANES), 1)
    lo = lane_row < NA_HEAD_DIM
    scale = NA_HEAD_DIM ** -0.5
    win_rows = NA_KH * GRID_W

    qi = lax.broadcasted_iota(jnp.int32, (GRID_W, LANES), 0)
    kcol = lax.broadcasted_iota(jnp.int32, (GRID_W, LANES), 1) & (GRID_W - 1)
    diff = kcol - qi + (NA_KW - 1)
    win = jnp.clip(qi - NA_KW // 2, 0, GRID_W - NA_KW)
    valid = (kcol >= win) & (kcol < win + NA_KW)

    def build(m, carry):
        for h in range(2):
            base = ((layer * NA_HEADS + 2 * pair + h) * RPB_ROWS + m) * RPB_COLS
            val = jnp.zeros((GRID_W, LANES), F32)
            for d in range(RPB_COLS):
                both = jnp.where(lo, rpb_ref[base + d], rpb_ref[base + RPB_COLS + d])
                val = jnp.where(diff == d, both, val)
            bias_s[m, h * GRID_W:(h + 1) * GRID_W, :] = jnp.where(valid, val, NEG_INF)
        return carry

    lax.fori_loop(0, BIAS_PAIRS, build, 0)

    vaug_s[:, :LANES] = v_ref[...]
    vaug_s[:, LANES:] = jnp.ones((T_ALL, LANES), BF16)

    def stack_heads(q):
        q = q * jnp.asarray(scale, q.dtype)
        zero = jnp.zeros_like(q)
        return jnp.concatenate([jnp.where(lo, q, zero), jnp.where(lo, zero, q)], axis=0)

    def softmax_pv(s_parts, v_parts, nq):
        tiles = [s[:, t * LANES:(t + 1) * LANES] for s in s_parts for t in range(s.shape[1] // LANES)]
        m = jnp.max(functools.reduce(jnp.maximum, tiles), axis=-1, keepdims=True)
        o2 = functools.reduce(jnp.add, [
            jnp.dot(jnp.exp(s - m).astype(BF16), v, preferred_element_type=F32)
            for s, v in zip(s_parts, v_parts)])
        o = o2[:, :LANES] / o2[:, LANES:]
        return jnp.where(lo, o[:nq], o[nq:])

    def latent_rows(b, r):
        cbase = CTX_BASE + b * CTX_LEN
        start = jnp.clip(r - NA_KH // 2, 0, GRID_ROWS - NA_KH)
        d0 = start - r + NA_KH - 1
        qrow = pl.multiple_of(b * SEQ + r * GRID_W, GRID_W)
        krow = pl.multiple_of(b * SEQ + start * GRID_W, GRID_W)
        q2 = stack_heads(q_ref[pl.ds(qrow, GRID_W), :])
        bias = jnp.concatenate([bias_s[d0 + 2 * t] for t in range(NA_KH // 2)], axis=1)
        s_loc = _nt_dot(q2, k_ref[pl.ds(krow, win_rows), :]) + bias
        s_ctx = _nt_dot(q2, k_ref[cbase:cbase + CTX_LEN, :])
        o = softmax_pv([s_loc, s_ctx],
                       [vaug_s[pl.ds(krow, win_rows), :], vaug_s[cbase:cbase + CTX_LEN, :]], GRID_W)
        o_ref[pl.ds(qrow, GRID_W), :] = o.astype(o_ref.dtype)

    def row_body(r, carry):
        for b in range(BATCH):
            latent_rows(b, r)
        return carry

    lax.fori_loop(0, GRID_ROWS, row_body, 0, unroll=16)

    for b in range(BATCH if ctx_queries else 0):
        cbase = CTX_BASE + b * CTX_LEN
        rows = slice(cbase, cbase + CTX_LEN)
        q2 = stack_heads(q_ref[rows, :])
        o = softmax_pv([_nt_dot(q2, k_ref[rows, :])], [vaug_s[rows, :]], CTX_LEN)
        o_ref[rows, :] = o.astype(o_ref.dtype)


def _attention(z, rpb_flat, w_conv_out, w_sgu_out, w_na_out, w_o, l, rows):
    steps = NA_HEADS // 2
    blk = lambda off: pl.BlockSpec((T_ALL, LANES), lambda p, off=off: (0, off // LANES + p))
    w_in = lambda k: pl.BlockSpec((None, k // steps, D_MODEL), lambda p: (l, p, 0))
    w_out = lambda k: pl.BlockSpec((k // steps, D_MODEL), lambda p: (p, 0))
    w_shape = lambda k: jax.ShapeDtypeStruct((k, D_MODEL), BF16)
    return pl.pallas_call(
        functools.partial(_attn_kernel, layer=l, ctx_queries=rows > T_LAT),
        grid=(steps,),
        in_specs=[pl.BlockSpec(memory_space=pltpu.SMEM), blk(Q_OFF), blk(K_OFF), blk(V_OFF),
                  w_in(CONV_DIM), w_in(SGU_DIM), w_in(NA_DIM), w_in(D_MODEL)],
        out_specs=[pl.BlockSpec((rows, LANES), lambda p: (0, p)),
                   w_out(CONV_DIM), w_out(SGU_DIM), w_out(NA_DIM), w_out(D_MODEL)],
        out_shape=[jax.ShapeDtypeStruct((rows, NA_DIM), BF16),
                   w_shape(CONV_DIM), w_shape(SGU_DIM), w_shape(NA_DIM), w_shape(D_MODEL)],
        scratch_shapes=[pltpu.VMEM((BIAS_PAIRS, 2 * GRID_W, LANES), F32),
                        pltpu.VMEM((T_ALL, 2 * LANES), BF16)],
        compiler_params=_params("arbitrary"),
        name="attention",
    )(rpb_flat, z, z, z, w_conv_out, w_sgu_out, w_na_out, w_o)


GATE_BLK = 1024
CAST_SLICES = 32


def _mix_kernel(*refs, tm, split):
    halves = D_MODEL // GATE_BLK
    u_refs = refs[:N_BRANCH]
    gate_refs = refs[N_BRANCH:N_BRANCH * (1 + halves)]
    w_refs = refs[N_BRANCH * (1 + halves):N_BRANCH * (2 + halves)]
    (wo_ref, bo_ref, *res_refs, gate1_ref, g_ref, sh_ref, sc_ref, w2_ref,
     x_out, h_out, w2b_ref) = refs[N_BRANCH * (2 + halves):]
    row0 = pl.program_id(0) * tm
    seg = _segment(row0)

    @pl.when(pl.program_id(0) < CAST_SLICES)
    def _():
        w2b_ref[...] = w2_ref[...].astype(BF16)

    y = bo_ref[...]
    for hf in range(halves):
        cols = slice(hf * GATE_BLK, (hf + 1) * GATE_BLK)
        merged = None
        for b in range(N_BRANCH):
            gate = _sigmoid(gate_refs[b * halves + hf][...].astype(F32))
            term = gate * jnp.dot(u_refs[b][...], w_refs[b][:, cols], preferred_element_type=F32)
            merged = term if merged is None else merged + term
        y = y + jnp.dot(merged.astype(BF16), wo_ref[cols, :], preferred_element_type=F32)
    res = _stream_tile(*res_refs, row0) if split else res_refs[0][...]
    x = res + gate1_ref[pl.ds(seg, 1), :] * y
    x_out[...] = x
    h = _rms(x, g_ref[...])
    h_out[...] = (h * (1.0 + sc_ref[pl.ds(seg, 1), :]) + sh_ref[pl.ds(seg, 1), :]).astype(BF16)


def _mix(z, u_branches, w_branches, wob, b_o, xs, norm2_g, mods, w_ff2, l, rows):
    tm = 256
    w2_rows = D_FF // CAST_SLICES
    w2_step = lambda i: jnp.minimum(i, CAST_SLICES - 1)
    split = isinstance(xs, tuple)
    halves = D_MODEL // GATE_BLK
    act = pl.BlockSpec((tm, CONV_DIM), lambda i: (i, 0))
    gate = lambda col: pl.BlockSpec((tm, GATE_BLK), lambda i: (i, col))
    gates = [gate(G_OFF // GATE_BLK + b * halves + hf) for b in range(N_BRANCH) for hf in range(halves)]
    resident = lambda k: pl.BlockSpec((k, D_MODEL), lambda i: (0, 0), pipeline_mode=pl.Buffered(1))
    row_blk = pl.BlockSpec((tm, D_MODEL), lambda i: (i, 0))
    bo_spec, bo = _layer_vec(b_o, l, D_MODEL)
    g_spec, g_arr = _layer_vec(norm2_g, l, D_MODEL)
    return pl.pallas_call(
        functools.partial(_mix_kernel, tm=tm, split=split),
        grid=(rows // tm,),
        in_specs=[act] * N_BRANCH + gates + [resident(CONV_DIM)] * N_BRANCH + [resident(D_MODEL), bo_spec]
        + (_split_stream_specs(tm) if split else [row_blk])
        + [_mod_spec(l, 2), g_spec, _mod_spec(l, 3), _mod_spec(l, 4),
           pl.BlockSpec((None, w2_rows, D_MODEL), lambda i: (l, w2_step(i), 0))],
        out_specs=[row_blk, row_blk, pl.BlockSpec((w2_rows, D_MODEL), lambda i: (w2_step(i), 0))],
        out_shape=[jax.ShapeDtypeStruct((rows, D_MODEL), F32), jax.ShapeDtypeStruct((rows, D_MODEL), BF16),
                   jax.ShapeDtypeStruct((D_FF, D_MODEL), BF16)],
        compiler_params=_params("arbitrary"),
        name="mix",
    )(*u_branches, *([z] * (N_BRANCH * halves)), *w_branches, wob, bo, *(xs if split else [xs]),
      mods, g_arr, mods, mods, w_ff2)


def kernel(x, c, ctx, c_ctx, w_mod, b_mod, norm1_g, norm2_g, w_in, b_in, conv_w, conv_b, conv_ln_g, conv_ln_b, w_conv_out, sgu_ln_g, sgu_ln_b, sgu_w, sgu_b, w_sgu_out, na_rpb, w_na_out, w_o, b_o, w_ff1, b_ff1, w_ff2, b_ff2, final_g):
    xs = (x.reshape(T_LAT, D_MODEL), ctx.reshape(T_CTX, D_MODEL))
    c_rows = jnp.zeros((MOD_ROWS, D_MODEL), F32).at[:BATCH].set(c).at[BATCH].set(c_ctx)
    mods = _mod_table(c_rows, w_mod, b_mod)
    rpb_flat = na_rpb.reshape(-1)

    for l in range(DEPTH):
        rows = T_LAT if l == DEPTH - 1 else T_ALL
        h = _mod_norm(xs, norm1_g, mods, l, 0, T_ALL)
        z = _matmul(h, w_in, b_in, l, tm=T_ALL // 4, tn=1024, out_dtype=BF16, name="in_proj")
        u_conv = _conv_branch(z, conv_w, conv_b, conv_ln_g, conv_ln_b, l, rows)
        u_sgu = _sgu_branch(z, sgu_ln_g, sgu_ln_b, sgu_w, sgu_b, l, rows)
        u_att, wcb, wsb, wab, wob = _attention(z, rpb_flat, w_conv_out, w_sgu_out, w_na_out, w_o, l, rows)
        xs, h2, w2b = _mix(z, (u_conv, u_sgu, u_att), (wcb, wsb, wab), wob, b_o, xs, norm2_g, mods,
                           w_ff2, l, rows)
        a = _matmul(h2, w_ff1, b_ff1, l, tm=rows // 4, tn=1024, out_dtype=BF16, act="relu2", name="ffn1")
        xs = _matmul(a, w2b, b_ff2, l, tm=512, tn=1024, out_dtype=F32, res=xs, mods=mods, k_gate=5,
                     name="ffn2")

    return _final_norm(xs, final_g).reshape(BATCH, SEQ, D_MODEL)
```
